```python
import jax, jax.numpy as jnp
from jax import lax
import numpy as np

D_MODEL = 1024
BATCH = 8
SEQ = 8192
DEPTH = 2

CTX_LEN = 256
GRID_W = 64
W_BRANCH = 512
N_BRANCH = 3
CHUNK = 128
A_GROUPS = 4
A_GW = W_BRANCH // A_GROUPS
B_BLOCKS = 8
B_BW = W_BRANCH // B_BLOCKS
CONV_W = 4
CONV_PAD_L = 2
LRU_C = 8.0
C_HEADS = 8
C_HD = W_BRANCH // C_HEADS
WIN_R = 8
WIN_C = 16
ROPE_BASE = 10000.0
IN_SPLITS = (W_BRANCH,) * 9 + (N_BRANCH * D_MODEL,)
N_IN = 9 * W_BRANCH + N_BRANCH * D_MODEL
ALPHA = (2 * DEPTH) ** 0.25
BETA = (8 * DEPTH) ** -0.25
LN_EPS = 1e-5

kernel_name = "hybrid_gmlp_rglru_natten_deepnorm"


def layer_norm(x, g, b):
    xf = x.astype(jnp.float32)
    mu = jnp.mean(xf, -1, keepdims=True)
    var = jnp.mean(jnp.square(xf - mu), -1, keepdims=True)
    y = (xf - mu) * lax.rsqrt(var + LN_EPS) * g.astype(jnp.float32) + b.astype(jnp.float32)
    return y.astype(x.dtype)


def split_cols(z):
    idx = [int(i) for i in np.cumsum(IN_SPLITS)[:-1]]
    return jnp.split(z, idx, axis=-1)


def heads(z):
    return z.reshape(*z.shape[:-1], C_HEADS, C_HD)


def chunk_sgu(u, v, ln_g, ln_b, w_s, b_s):
    bsz, L, _ = v.shape
    v = layer_norm(v, ln_g, ln_b).reshape(bsz, L // CHUNK, CHUNK, A_GROUPS, A_GW)
    v = jnp.einsum("gpq,bnqgc->bnpgc", w_s, v) + b_s.T[:, :, None]
    return u * v.reshape(bsz, L, W_BRANCH)


def depthwise_conv_centred(x, w, b):
    L = x.shape[1]
    xp = jnp.pad(x, ((0, 0), (CONV_PAD_L, CONV_W - 1 - CONV_PAD_L), (0, 0)))
    return b + sum(xp[:, j:j + L] * w[j] for j in range(CONV_W))


def rglru_coeffs(x, wa, ba, wx, bx, lam):
    bsz, L, _ = x.shape
    xb = x.reshape(bsz, L, B_BLOCKS, B_BW)
    r = jax.nn.sigmoid(jnp.einsum("blhi,hij->blhj", xb, wa).reshape(bsz, L, W_BRANCH) + ba)
    i = jax.nn.sigmoid(jnp.einsum("blhi,hij->blhj", xb, wx).reshape(bsz, L, W_BRANCH) + bx)
    log_a = -LRU_C * r.astype(jnp.float32) * jax.nn.softplus(-lam.astype(jnp.float32))
    a = jnp.exp(log_a)
    b = jnp.sqrt(-jnp.expm1(2.0 * log_a)) * (i * x).astype(jnp.float32)
    return a, b


def linear_scan(a, b, h0, reverse):
    idx = -1 if reverse else 0
    b = b.at[:, idx].add(a[:, idx] * h0)

    def combine(left, right):
        al, bl = left
        ar, br = right
        return ar * al, ar * bl + br

    _, h = lax.associative_scan(combine, (a, b), reverse=reverse, axis=1)
    return h


def rglru_bidir(x_lat, x_ctx, conv_w, conv_b, wa, ba, wx, bx, lam, with_ctx):
    xl = depthwise_conv_centred(x_lat, conv_w, conv_b)
    xc = depthwise_conv_centred(x_ctx, conv_w, conv_b)
    h0 = jnp.zeros((x_lat.shape[0], W_BRANCH), jnp.float32)
    ys_lat, ys_ctx = [], []
    for d, rev in enumerate((False, True)):
        a_c, b_c = rglru_coeffs(xc, wa[d], ba[d], wx[d], bx[d], lam[d])
        h_c = linear_scan(a_c, b_c, h0, rev)
        h_fin = h_c[:, 0] if rev else h_c[:, -1]
        a_l, b_l = rglru_coeffs(xl, wa[d], ba[d], wx[d], bx[d], lam[d])
        ys_lat.append(linear_scan(a_l, b_l, h_fin, rev))
        if with_ctx:
            ys_ctx.append(h_c)
    y_lat = (ys_lat[0] + ys_lat[1]).astype(x_lat.dtype)
    y_ctx = (ys_ctx[0] + ys_ctx[1]).astype(x_ctx.dtype) if with_ctx else None
    return y_lat, y_ctx


def rope_2d(x, rows, cols):
    half = C_HD // 2
    quarter = half // 2
    inv_freq = ROPE_BASE ** (-jnp.arange(quarter, dtype=jnp.float32) / quarter)

    def rotate(xa, p):
        ang = p.astype(jnp.float32)[:, None] * inv_freq
        cos = jnp.cos(ang)[None, :, None, :]
        sin = jnp.sin(ang)[None, :, None, :]
        xa = xa.astype(jnp.float32)
        x1, x2 = xa[..., :quarter], xa[..., quarter:]
        return jnp.concatenate([x1 * cos - x2 * sin, x1 * sin + x2 * cos], -1)

    out = jnp.concatenate([rotate(x[..., :half], rows), rotate(x[..., half:], cols)], -1)
    return out.astype(x.dtype)


def neighbourhood_attention(q, k, v, k_ctx, v_ctx, rpb):
    bsz, L, nh, hd = q.shape
    rows = L // GRID_W
    wr = min(WIN_R, rows)
    scale = hd ** -0.5
    qg = q.reshape(bsz, rows, GRID_W, nh, hd)
    kg = k.reshape(bsz, rows, GRID_W, nh, hd)
    vg = v.reshape(bsz, rows, GRID_W, nh, hd)
    qc = jnp.arange(GRID_W)
    c0 = jnp.clip(qc - WIN_C // 2, 0, GRID_W - WIN_C)
    col_idx = c0[:, None] + jnp.arange(WIN_C)[None, :]
    col_bias_idx = col_idx - qc[:, None] + (WIN_C - 1)

    def row_block(r):
        r0 = jnp.clip(r - wr // 2, 0, rows - wr)
        k_rows = lax.dynamic_slice_in_dim(kg, r0, wr, axis=1)
        v_rows = lax.dynamic_slice_in_dim(vg, r0, wr, axis=1)
        k_win = k_rows[:, :, col_idx]
        v_win = v_rows[:, :, col_idx]
        q_r = lax.dynamic_index_in_dim(qg, r, axis=1, keepdims=False)
        s_loc = jnp.einsum("bqhd,bsqchd->bhqsc", q_r, k_win).astype(jnp.float32) * scale
        row_off = r0 + jnp.arange(wr) - r + (WIN_R - 1)
        bias = rpb[:, row_off[None, :, None], col_bias_idx[:, None, :]]
        s_loc = s_loc + bias[None].astype(jnp.float32)
        s_ctx = jnp.einsum("bqhd,bkhd->bhqk", q_r, k_ctx).astype(jnp.float32) * scale
        s = jnp.concatenate([s_loc.reshape(bsz, nh, GRID_W, wr * WIN_C), s_ctx], -1)
        p = jax.nn.softmax(s, -1).astype(v.dtype)
        p_loc = p[..., :wr * WIN_C].reshape(bsz, nh, GRID_W, wr, WIN_C)
        p_ctx = p[..., wr * WIN_C:]
        return (jnp.einsum("bhqsc,bsqchd->bqhd", p_loc, v_win)
                + jnp.einsum("bhqk,bkhd->bqhd", p_ctx, v_ctx))

    out = lax.map(row_block, jnp.arange(rows))
    return out.transpose(1, 0, 2, 3, 4).reshape(bsz, L, nh * hd)


def context_attention(q, k, v):
    s = jnp.einsum("bqhd,bkhd->bhqk", q, k).astype(jnp.float32) * C_HD ** -0.5
    p = jax.nn.softmax(s, -1).astype(v.dtype)
    o = jnp.einsum("bhqk,bkhd->bqhd", p, v)
    return o.reshape(*o.shape[:2], W_BRANCH)


def merge_project(x, ys, g_m, gate, w_br, w_out, ln_g, ln_b):
    g = jax.nn.sigmoid(g_m)
    m = sum(g[..., n * D_MODEL:(n + 1) * D_MODEL] * (ys[n] @ w_br[n]) for n in range(N_BRANCH))
    return layer_norm(ALPHA * x + gate * (m @ w_out), ln_g, ln_b)


def setup_inputs(seed: int = 0) -> dict:
    key = jax.random.key(seed)
    ks = jax.random.split(key, 32)
    f32 = jnp.float32
    nrm = lambda k, shape, s: (jax.random.normal(k, shape, f32) * s).astype(f32)
    a0 = jax.random.uniform(ks[20], (DEPTH, 2, W_BRANCH), f32, minval=0.9, maxval=0.999)
    a_base = a0 ** (1.0 / LRU_C)
    lam = jnp.log(a_base) - jnp.log1p(-a_base)
    return {
        "x": nrm(ks[0], (BATCH, SEQ, D_MODEL), 1.0),
        "c": nrm(ks[1], (BATCH, D_MODEL), 1.0),
        "ctx": nrm(ks[2], (BATCH, CTX_LEN, D_MODEL), 1.0),
        "c_ctx": nrm(ks[3], (D_MODEL,), 1.0),
        "w_ada": nrm(ks[4], (DEPTH, D_MODEL, 3 * D_MODEL), 0.3 * D_MODEL ** -0.5),
        "b_ada": nrm(ks[5], (DEPTH, 3 * D_MODEL), 0.02),
        "w_in": nrm(ks[6], (DEPTH, D_MODEL, N_IN), D_MODEL ** -0.5),
        "b_in": nrm(ks[7], (DEPTH, N_IN), 0.02),
        "sgu_ln_g": 1.0 + nrm(ks[8], (DEPTH, W_BRANCH), 0.02),
        "sgu_ln_b": nrm(ks[9], (DEPTH, W_BRANCH), 0.02),
        "w_s": nrm(ks[10], (DEPTH, A_GROUPS, CHUNK, CHUNK), CHUNK ** -0.5),
        "b_s": 1.0 + nrm(ks[11], (DEPTH, A_GROUPS, CHUNK), 0.1),
        "conv_w": nrm(ks[12], (DEPTH, CONV_W, W_BRANCH), CONV_W ** -0.5),
        "conv_b": nrm(ks[13], (DEPTH, W_BRANCH), 0.02),
        "lru_wa": nrm(ks[14], (DEPTH, 2, B_BLOCKS, B_BW, B_BW), B_BW ** -0.5),
        "lru_ba": nrm(ks[15], (DEPTH, 2, W_BRANCH), 0.02),
        "lru_wx": nrm(ks[16], (DEPTH, 2, B_BLOCKS, B_BW, B_BW), B_BW ** -0.5),
        "lru_bx": nrm(ks[17], (DEPTH, 2, W_BRANCH), 0.02),
        "lru_lam": lam.astype(f32),
        "rpb": nrm(ks[18], (DEPTH, C_HEADS, 2 * WIN_R - 1, 2 * WIN_C - 1), 0.1),
        "w_br": nrm(ks[19], (DEPTH, N_BRANCH, W_BRANCH, D_MODEL), BETA * W_BRANCH ** -0.5),
        "w_out": nrm(ks[21], (DEPTH, D_MODEL, D_MODEL), BETA * D_MODEL ** -0.5),
        "ln_g": 1.0 + nrm(ks[22], (DEPTH, D_MODEL), 0.02),
        "ln_b": nrm(ks[23], (DEPTH, D_MODEL), 0.02),
    }


def reference(x, c, ctx, c_ctx, w_ada, b_ada, w_in, b_in, sgu_ln_g, sgu_ln_b, w_s, b_s,
              conv_w, conv_b, lru_wa, lru_ba, lru_wx, lru_bx, lru_lam, rpb, w_br, w_out,
              ln_g, ln_b):
    L = x.shape[1]
    pos = jnp.arange(L)
    rows_pos, cols_pos = pos // GRID_W, pos % GRID_W
    xc = ctx
    sc = jax.nn.silu(c)
    scc = jax.nn.silu(c_ctx)
    gelu = lambda t: jax.nn.gelu(t, approximate=False)
    for l in range(DEPTH):
        with_ctx = l < DEPTH - 1
        shift, scale, gate = jnp.split(sc @ w_ada[l] + b_ada[l], 3, axis=-1)
        shift_c, scale_c, gate_c = jnp.split(scc @ w_ada[l] + b_ada[l], 3, axis=-1)
        u = x * (1.0 + scale[:, None]) + shift[:, None]
        uc = xc * (1.0 + scale_c) + shift_c
        a_u, a_v, a_g, b_x, b_g, c_q, c_k, c_v, c_g, g_m = split_cols(u @ w_in[l] + b_in[l])
        a_uc, a_vc, a_gc, b_xc, b_gc, c_qc, c_kc, c_vc, c_gc, g_mc = split_cols(uc @ w_in[l] + b_in[l])

        y_a = chunk_sgu(gelu(a_u), gelu(a_v), sgu_ln_g[l], sgu_ln_b[l], w_s[l], b_s[l]) * jax.nn.silu(a_g)
        y_b, y_bc = rglru_bidir(b_x, b_xc, conv_w[l], conv_b[l], lru_wa[l], lru_ba[l],
                                lru_wx[l], lru_bx[l], lru_lam[l], with_ctx)
        y_b = y_b * jax.nn.silu(b_g)
        q = rope_2d(heads(c_q), rows_pos, cols_pos)
        k = rope_2d(heads(c_k), rows_pos, cols_pos)
        k_ctx, v_ctx = heads(c_kc), heads(c_vc)
        y_c = neighbourhood_attention(q, k, heads(c_v), k_ctx, v_ctx, rpb[l]) * jax.nn.silu(c_g)

        x_new = merge_project(x, (y_a, y_b, y_c), g_m, gate[:, None], w_br[l], w_out[l], ln_g[l], ln_b[l])
        if with_ctx:
            y_ac = chunk_sgu(gelu(a_uc), gelu(a_vc), sgu_ln_g[l], sgu_ln_b[l], w_s[l], b_s[l]) * jax.nn.silu(a_gc)
            y_bc = y_bc * jax.nn.silu(b_gc)
            y_cc = context_attention(heads(c_qc), k_ctx, v_ctx) * jax.nn.silu(c_gc)
            xc = merge_project(xc, (y_ac, y_bc, y_cc), g_mc, gate_c, w_br[l], w_out[l], ln_g[l], ln_b[l])
        x = x_new
    return x
```

```python
import functools

import numpy as np
import jax
import jax.numpy as jnp
from jax import lax
from jax.experimental import pallas as pl
from jax.experimental.pallas import tpu as pltpu

F32 = jnp.float32
BF16 = jnp.bfloat16

GRID_W = 64
W_BRANCH = 512
N_BRANCH = 3
CHUNK = 128
A_GROUPS = 4
B_BLOCKS = 8
B_BW = W_BRANCH // B_BLOCKS
CONV_W = 4
CONV_PAD_L = 2
LRU_C = 8.0
C_HEADS = 8
C_HD = W_BRANCH // C_HEADS
WIN_R = 8
WIN_C = 16
ROPE_BASE = 10000.0
LN_EPS = 1e-5
NEG_BIG = -1e30

LANES = 128
HALO = 8
GATE_GROUP = 256
SCAN_TILE = 128
MAIN_ROWS = 8
VMEM_LIMIT = 60 * 1024 * 1024

_OFF_AU, _OFF_AV, _OFF_AG, _OFF_BG, _OFF_CQ, _OFF_CG, _OFF_GM = 0, 512, 1024, 1536, 2048, 2560, 3072


def _dotf(a, b):
    return jnp.dot(a, b, preferred_element_type=F32)


def _dot_nt(a, b):
    return lax.dot_general(a, b, (((1,), (1,)), ((), ())), preferred_element_type=F32)


def _sigmoid(x):
    return 1.0 / (1.0 + jnp.exp(-x))


def _silu(x):
    return x * _sigmoid(x)


def _gelu(x):
    return 0.5 * x * (1.0 + lax.erf(x * 0.7071067811865476))


def _softplus(x):
    return jnp.maximum(x, 0.0) + jnp.log1p(jnp.exp(-jnp.abs(x)))


def _ln(x, g, b):
    mu = jnp.mean(x, axis=-1, keepdims=True)
    xc = x - mu
    var = jnp.mean(xc * xc, axis=-1, keepdims=True)
    return xc * lax.rsqrt(var + LN_EPS) * g + b


def _rope(x, cos, sin_lo, sin_hi):
    outs = []
    for s in range(x.shape[1] // LANES):
        xs = x[:, s * LANES:(s + 1) * LANES]
        outs.append(xs * cos + pltpu.roll(xs, LANES - 16, 1) * sin_lo + pltpu.roll(xs, 16, 1) * sin_hi)
    return jnp.concatenate(outs, axis=1)


def _conv_from_ext(ext_ref, n, cw, cb):
    acc = cb
    for j in range(CONV_W):
        acc = acc + cw[j:j + 1, :] * ext_ref[pl.ds(HALO - CONV_PAD_L + j, n), :]
    return acc


def _lru_coeffs(xl, gw0, gw1, ba, bx, lam):
    xb = xl.astype(BF16)
    o0 = _dotf(xb[:, :GATE_GROUP], gw0)
    o1 = _dotf(xb[:, GATE_GROUP:], gw1)
    r = _sigmoid(jnp.concatenate([o0[:, :GATE_GROUP], o1[:, :GATE_GROUP]], axis=1) + ba)
    i = _sigmoid(jnp.concatenate([o0[:, GATE_GROUP:], o1[:, GATE_GROUP:]], axis=1) + bx)
    log_a = (-LRU_C) * r * _softplus(-lam)
    a = jnp.exp(log_a)
    t = jnp.tanh(log_a)
    one_minus_a2 = (-2.0 * t) / (1.0 - t)
    return a, jnp.sqrt(one_minus_a2) * (i * xl)


def _branch_a(zu, zv, zg, lng, lnb, ws_ref, bs_tab):
    gu = _gelu(zu)
    gv = _ln(_gelu(zv), lng, lnb).astype(BF16)
    rows = []
    for n in range(zu.shape[0] // CHUNK):
        cols = [_dotf(ws_ref[g], gv[n * CHUNK:(n + 1) * CHUNK, g * LANES:(g + 1) * LANES])
                for g in range(A_GROUPS)]
        rows.append(jnp.concatenate(cols, axis=1) + bs_tab)
    return gu * jnp.concatenate(rows, axis=0) * _silu(zg)


def _head_mask(rows, second):
    lane = lax.broadcasted_iota(jnp.int32, (rows, LANES), 1)
    return (lane >= C_HD) if second else (lane < C_HD)


def _ada_kernel(cc_ref, w_ref, b_ref, o_ref):
    s = _silu(cc_ref[...])
    o_ref[0] = jnp.dot(s, w_ref[0], preferred_element_type=F32, precision=lax.Precision.HIGHEST) + b_ref[0]


def _ada_call(cc, w_ada, b_ada):
    depth, d, n3 = w_ada.shape
    nb = n3 // d
    return pl.pallas_call(
        _ada_kernel,
        grid=(depth, nb),
        in_specs=[pl.BlockSpec(cc.shape, lambda l, j: (0, 0)),
                  pl.BlockSpec((1, d, d), lambda l, j: (l, 0, j)),
                  pl.BlockSpec((1, 1, d), lambda l, j: (l, 0, j))],
        out_specs=pl.BlockSpec((1, cc.shape[0], d), lambda l, j: (l, 0, j)),
        out_shape=jax.ShapeDtypeStruct((depth, cc.shape[0], n3), F32),
        compiler_params=pltpu.CompilerParams(dimension_semantics=("arbitrary", "arbitrary"),
                                             vmem_limit_bytes=VMEM_LIMIT),
        name="adaln_modulation",
    )(cc, w_ada, b_ada.reshape(depth, 1, n3))


def _merge_tail(x, u, gate, ys, w2_ref, b2_ref, wbr_ref, wout_ref, og, ob, alpha):
    m = None
    for n in range(N_BRANCH):
        c0 = _OFF_GM + n * x.shape[1]
        g = _sigmoid(_dotf(u, w2_ref[:, c0:c0 + x.shape[1]]) + b2_ref[:, c0:c0 + x.shape[1]])
        t = g * _dotf(ys[n].astype(BF16), wbr_ref[n])
        m = t if m is None else m + t
    return _ln(alpha * x + gate * _dotf(m.astype(BF16), wout_ref[...]), og, ob)


def _ctx_kernel(with_ctx, alpha, n_batch, *refs):
    (xc_ref, mod_ref, w1_ref, b1_ref, cw_ref, cb_ref, gw_ref, gba_ref, gbx_ref, lam_ref) = refs[:10]
    if with_ctx:
        (w2_ref, b2_ref, lng_ref, lnb_ref, ws_ref, bs_ref, wbr_ref, wout_ref, og_ref, ob_ref) = refs[10:20]
        k_ref, v_ref, hfin_ref, xo_ref = refs[20:24]
        ext_scr, af_scr, bf_scr, ar_scr, br_scr, hf_scr, hr_scr = refs[24:]
    else:
        k_ref, v_ref, hfin_ref = refs[10:13]
        ext_scr, af_scr, bf_scr, ar_scr, br_scr, hf_scr, hr_scr = refs[13:]
    lc, d = xc_ref.shape[1], xc_ref.shape[2]
    w = W_BRANCH

    xc = xc_ref[0]
    mod = mod_ref[n_batch:n_batch + 1, :]
    shift, scale, gate = mod[:, 0:d], mod[:, d:2 * d], mod[:, 2 * d:3 * d]
    u = (xc * (1.0 + scale) + shift).astype(BF16)

    z1 = _dotf(u, w1_ref[...]) + b1_ref[...]
    bx = z1[:, 0:w]
    kc = z1[:, w:2 * w].astype(BF16)
    vc = z1[:, 2 * w:3 * w].astype(BF16)
    k_ref[0] = kc
    v_ref[0] = vc

    zeros = jnp.zeros((HALO, w), F32)
    ext_scr[pl.ds(0, HALO), :] = zeros
    ext_scr[pl.ds(HALO, lc), :] = bx
    ext_scr[pl.ds(HALO + lc, HALO), :] = zeros
    xl = _conv_from_ext(ext_scr, lc, cw_ref[...], cb_ref[...])

    a_f, b_f = _lru_coeffs(xl, gw_ref[0, 0], gw_ref[0, 1], gba_ref[0], gbx_ref[0], lam_ref[0])
    a_r, b_r = _lru_coeffs(xl, gw_ref[1, 0], gw_ref[1, 1], gba_ref[1], gbx_ref[1], lam_ref[1])
    af_scr[...] = a_f
    bf_scr[...] = b_f
    ar_scr[...] = a_r
    br_scr[...] = b_r

    def body(t, carry):
        hf, hr = carry
        tr = lc - 1 - t
        hf = af_scr[pl.ds(t, 1), :] * hf + bf_scr[pl.ds(t, 1), :]
        hr = ar_scr[pl.ds(tr, 1), :] * hr + br_scr[pl.ds(tr, 1), :]
        hf_scr[pl.ds(t, 1), :] = hf
        hr_scr[pl.ds(tr, 1), :] = hr
        return hf, hr

    h0 = jnp.zeros((1, w), F32)
    hf, hr = lax.fori_loop(0, lc, body, (h0, h0))
    hfin_ref[0, 0:1, :] = hf
    hfin_ref[0, 1:2, :] = hr

    if not with_ctx:
        return

    def proj(c0, n):
        return _dotf(u, w2_ref[:, c0:c0 + n]) + b2_ref[:, c0:c0 + n]

    y_a = _branch_a(proj(_OFF_AU, w), proj(_OFF_AV, w), proj(_OFF_AG, w),
                    lng_ref[...], lnb_ref[...], ws_ref, bs_ref[...])
    y_b = (hf_scr[...] + hr_scr[...]) * _silu(proj(_OFF_BG, w))

    q = (proj(_OFF_CQ, w) * (C_HD ** -0.5)).astype(BF16)
    outs = []
    for p in range(w // LANES):
        qp = q[:, p * LANES:(p + 1) * LANES]
        kp = kc[:, p * LANES:(p + 1) * LANES]
        vp = vc[:, p * LANES:(p + 1) * LANES]
        acc = None
        for hh in range(2):
            msk = _head_mask(lc, hh == 1)
            s = _dot_nt(jnp.where(msk, qp, jnp.zeros_like(qp)), kp)
            e = jnp.exp(s - jnp.max(s, axis=-1, keepdims=True))
            o = _dotf(e.astype(BF16), vp) / jnp.sum(e, axis=-1, keepdims=True)
            acc = o if acc is None else jnp.where(msk, o, acc)
        outs.append(acc)
    y_c = jnp.concatenate(outs, axis=1) * _silu(proj(_OFF_CG, w))

    xo_ref[0] = _merge_tail(xc, u, gate, (y_a, y_b, y_c), w2_ref, b2_ref, wbr_ref, wout_ref,
                            og_ref[...], ob_ref[...], alpha)


def _const_spec(shape):
    nd = len(shape)
    return pl.BlockSpec(shape, lambda *_: (0,) * nd, pipeline_mode=pl.Buffered(1))


def _ctx_call(with_ctx, alpha, xc, mod, p):
    bsz, lc, d = xc.shape
    w = W_BRANCH
    ins = [xc, mod, p["w1"], p["b1"], p["conv_w"], p["conv_b"], p["gw"], p["gba"], p["gbx"], p["lam"]]
    in_specs = [pl.BlockSpec((1, lc, d), lambda b: (b, 0, 0))] + [_const_spec(a.shape) for a in ins[1:]]
    out_shape = [jax.ShapeDtypeStruct((bsz, lc, w), BF16), jax.ShapeDtypeStruct((bsz, lc, w), BF16),
                 jax.ShapeDtypeStruct((bsz, 2, w), F32)]
    out_specs = [pl.BlockSpec((1, lc, w), lambda b: (b, 0, 0)), pl.BlockSpec((1, lc, w), lambda b: (b, 0, 0)),
                 pl.BlockSpec((1, 2, w), lambda b: (b, 0, 0))]
    if with_ctx:
        extra = [p["w2"], p["b2"], p["sgu_g"], p["sgu_b"], p["ws"], p["bs_tab"], p["wbr"], p["wout"],
                 p["ln_g"], p["ln_b"]]
        ins += extra
        in_specs += [_const_spec(a.shape) for a in extra]
        out_shape.append(jax.ShapeDtypeStruct((bsz, lc, d), F32))
        out_specs.append(pl.BlockSpec((1, lc, d), lambda b: (b, 0, 0)))
    scratch = [pltpu.VMEM((lc + 2 * HALO, w), F32)] + [pltpu.VMEM((lc, w), F32) for _ in range(6)]
    return pl.pallas_call(
        functools.partial(_ctx_kernel, with_ctx, alpha, bsz),
        grid=(bsz,),
        in_specs=in_specs,
        out_specs=out_specs,
        out_shape=out_shape,
        scratch_shapes=scratch,
        compiler_params=pltpu.CompilerParams(dimension_semantics=("arbitrary",), vmem_limit_bytes=VMEM_LIMIT),
        name="context_layer" if with_ctx else "context_kv_state",
    )(*ins)


def _scan_kernel(x_ref, xp_ref, xn_ref, mod_ref, w1_ref, b1_ref, wkv_ref, bkv_ref, cw_ref, cb_ref, gw_ref, gba_ref,
                 gbx_ref, lam_ref, h0_ref, cos_ref, slo_ref, shi_ref,
                 h_ref, kv_ref,
                 a_scr, b_scr, ext_scr, hst_scr):
    phase = pl.program_id(0)
    i = pl.program_id(1)
    nt = pl.num_programs(1)
    tile = jnp.where(phase == 0, i, nt - 1 - i)
    nb, tl, d = x_ref.shape
    w = W_BRANCH

    def modulate(xv, b):
        return (xv * (1.0 + mod_ref[b:b + 1, d:2 * d]) + mod_ref[b:b + 1, 0:d]).astype(BF16)

    u = jnp.concatenate([modulate(x_ref[b], b) for b in range(nb)], axis=0)
    u_halo = jnp.concatenate([modulate(xp_ref[b], b) for b in range(nb)]
                             + [modulate(xn_ref[b], b) for b in range(nb)], axis=0)

    bx = _dotf(u, w1_ref[...]) + b1_ref[...]
    bx_halo = _dotf(u_halo, w1_ref[...]) + b1_ref[...]
    bx_prev = jnp.where(tile == 0, 0.0, bx_halo[0:nb * HALO])
    bx_next = jnp.where(tile == nt - 1, 0.0, bx_halo[nb * HALO:2 * nb * HALO])

    kv = _dotf(u, wkv_ref[0]) + bkv_ref[0]
    cos, slo, shi = cos_ref[0], slo_ref[0], shi_ref[0]
    for b in range(nb):
        kv_ref[0, b] = _rope(kv[b * tl:(b + 1) * tl], cos, slo, shi).astype(BF16)

    cw, cb = cw_ref[...], cb_ref[...]
    xls = []
    for b in range(nb):
        ext_scr[b, pl.ds(0, HALO), :] = bx_prev[b * HALO:(b + 1) * HALO]
        ext_scr[b, pl.ds(HALO, tl), :] = bx[b * tl:(b + 1) * tl]
        ext_scr[b, pl.ds(HALO + tl, HALO), :] = bx_next[b * HALO:(b + 1) * HALO]
        xls.append(_conv_from_ext(ext_scr.at[b], tl, cw, cb))
    xl = jnp.concatenate(xls, axis=0)

    a, bco = _lru_coeffs(xl, gw_ref[0, 0], gw_ref[0, 1], gba_ref[0], gbx_ref[0], lam_ref[0])
    a_scr[...] = a
    b_scr[...] = bco

    @pl.when(i == 0)
    def _():
        hst_scr[...] = h0_ref[0]

    def body(t, hs):
        tt = jnp.where(phase == 0, t, tl - 1 - t)
        new = []
        for b in range(nb):
            row = b * tl + tt
            h = a_scr[pl.ds(row, 1), :] * hs[b] + b_scr[pl.ds(row, 1), :]
            h_ref[0, b, pl.ds(tt, 1), :] = h
            new.append(h)
        return tuple(new)

    hs = lax.fori_loop(0, tl, body, tuple(hst_scr[pl.ds(b, 1), :] for b in range(nb)))
    for b in range(nb):
        hst_scr[pl.ds(b, 1), :] = hs[b]


def _scan_call(x, mod, h0, p, tabs):
    bsz, seq, d = x.shape
    w = W_BRANCH
    tl = SCAN_TILE
    nt = seq // tl
    hb = tl // HALO
    n_halo = seq // HALO

    def tile_of(ph, i):
        return jnp.where(ph == 0, i, nt - 1 - i)

    def dir_spec(a):
        nd = a.ndim
        return pl.BlockSpec((1,) + a.shape[1:], lambda ph, i: (ph,) + (0,) * (nd - 1))

    in_specs = [
        pl.BlockSpec((bsz, tl, d), lambda ph, i: (0, tile_of(ph, i), 0)),
        pl.BlockSpec((bsz, HALO, d), lambda ph, i: (0, jnp.maximum(tile_of(ph, i) * hb - 1, 0), 0)),
        pl.BlockSpec((bsz, HALO, d), lambda ph, i: (0, jnp.minimum((tile_of(ph, i) + 1) * hb, n_halo - 1), 0)),
        _const_spec(mod.shape), _const_spec(p["wbx"].shape), _const_spec(p["bbx"].shape),
        dir_spec(p["wkv"]), dir_spec(p["bkv"]),
        _const_spec(p["conv_w"].shape), _const_spec(p["conv_b"].shape),
        dir_spec(p["gw"]), dir_spec(p["gba"]), dir_spec(p["gbx"]), dir_spec(p["lam"]), dir_spec(h0),
        pl.BlockSpec((1, tl, LANES), lambda ph, i: (ph, tile_of(ph, i), 0)),
        pl.BlockSpec((1, tl, LANES), lambda ph, i: (ph, tile_of(ph, i), 0)),
        pl.BlockSpec((1, tl, LANES), lambda ph, i: (ph, tile_of(ph, i), 0)),
    ]
    out_specs = [pl.BlockSpec((1, bsz, tl, w), lambda ph, i: (ph, 0, tile_of(ph, i), 0)),
                 pl.BlockSpec((1, bsz, tl, w), lambda ph, i: (ph, 0, tile_of(ph, i), 0))]
    out_shape = [jax.ShapeDtypeStruct((2, bsz, seq, w), F32), jax.ShapeDtypeStruct((2, bsz, seq, w), BF16)]
    scratch = [pltpu.VMEM((bsz * tl, w), F32), pltpu.VMEM((bsz * tl, w), F32),
               pltpu.VMEM((bsz, tl + 2 * HALO, w), F32), pltpu.VMEM((bsz, w), F32)]
    return pl.pallas_call(
        _scan_kernel,
        grid=(2, nt),
        in_specs=in_specs,
        out_specs=out_specs,
        out_shape=out_shape,
        scratch_shapes=scratch,
        compiler_params=pltpu.CompilerParams(dimension_semantics=("arbitrary", "arbitrary"),
                                             vmem_limit_bytes=VMEM_LIMIT),
        name="latent_scan",
    )(x, x, x, mod, p["wbx"], p["bbx"], p["wkv"], p["bkv"], p["conv_w"], p["conv_b"], p["gw"], p["gba"], p["gbx"],
      p["lam"], h0, tabs["cos2"], tabs["sin_lo2"], tabs["sin_hi2"])


def _main_kernel(alpha, x_ref, h_ref, kp_ref, kc_ref, kn_ref, vp_ref, vc_ref, vn_ref, kx_ref, vx_ref, mod_ref,
                 w2_ref, b2_ref, lng_ref, lnb_ref, ws_ref, bs_ref, wbr_ref, wout_ref, og_ref, ob_ref,
                 cos_ref, slo_ref, shi_ref, bm_ref,
                 o_ref,
                 kbuf, vbuf, q_scr, yc_scr):
    b = pl.program_id(0)
    i = pl.program_id(1)
    nt = pl.num_programs(1)
    tl, d = x_ref.shape[1], x_ref.shape[2]
    w = W_BRANCH
    half = kp_ref.shape[2]
    n_rows = tl // GRID_W

    x = x_ref[0]
    mod = mod_ref[pl.ds(b, 1), :]
    shift, scale, gate = mod[:, 0:d], mod[:, d:2 * d], mod[:, 2 * d:3 * d]
    u = (x * (1.0 + scale) + shift).astype(BF16)

    def proj(c0, n):
        return _dotf(u, w2_ref[:, c0:c0 + n]) + b2_ref[:, c0:c0 + n]

    y_a = _branch_a(proj(_OFF_AU, w), proj(_OFF_AV, w), proj(_OFF_AG, w),
                    lng_ref[...], lnb_ref[...], ws_ref, bs_ref[...])
    y_b = (h_ref[0, 0] + h_ref[1, 0]) * _silu(proj(_OFF_BG, w))

    q_scr[...] = (_rope(proj(_OFF_CQ, w), cos_ref[...], slo_ref[...], shi_ref[...]) * (C_HD ** -0.5)).astype(BF16)
    kbuf[pl.ds(0, half), :] = kp_ref[0, 0]
    kbuf[pl.ds(half, tl), :] = kc_ref[0, 0]
    kbuf[pl.ds(half + tl, half), :] = kn_ref[0, 0]
    vbuf[pl.ds(0, half), :] = vp_ref[0, 0]
    vbuf[pl.ds(half, tl), :] = vc_ref[0, 0]
    vbuf[pl.ds(half + tl, half), :] = vn_ref[0, 0]
    lo = jnp.where(i == 0, WIN_R // 2, 0)
    hi = jnp.where(i == nt - 1, WIN_R // 2, n_rows - 1)
    n_win = WIN_R * GRID_W

    def row_body(j, carry):
        off = jnp.minimum(jnp.maximum(j, lo), hi)
        var = j + WIN_R // 2 - off
        k0 = pl.multiple_of(off * GRID_W, GRID_W)
        q0 = pl.multiple_of(j * GRID_W, GRID_W)
        for p in range(w // LANES):
            ls = slice(p * LANES, (p + 1) * LANES)
            qp = q_scr[pl.ds(q0, GRID_W), ls]
            kw = kbuf[pl.ds(k0, n_win), ls]
            vw = vbuf[pl.ds(k0, n_win), ls]
            kx = kx_ref[0, :, ls]
            vx = vx_ref[0, :, ls]
            acc = None
            for hh in range(2):
                msk = _head_mask(GRID_W, hh == 1)
                qm = jnp.where(msk, qp, jnp.zeros_like(qp))
                s_loc = _dot_nt(qm, kw) + bm_ref[var, 2 * p + hh]
                s_ctx = _dot_nt(qm, kx)
                mx = jnp.maximum(jnp.max(s_loc, axis=-1, keepdims=True), jnp.max(s_ctx, axis=-1, keepdims=True))
                e_loc = jnp.exp(s_loc - mx)
                e_ctx = jnp.exp(s_ctx - mx)
                den = jnp.sum(e_loc, axis=-1, keepdims=True) + jnp.sum(e_ctx, axis=-1, keepdims=True)
                o = (_dotf(e_loc.astype(BF16), vw) + _dotf(e_ctx.astype(BF16), vx)) / den
                acc = o if acc is None else jnp.where(msk, o, acc)
            yc_scr[pl.ds(q0, GRID_W), ls] = acc
        return carry

    lax.fori_loop(0, n_rows, row_body, 0)
    y_c = yc_scr[...] * _silu(proj(_OFF_CG, w))

    o_ref[0] = _merge_tail(x, u, gate, (y_a, y_b, y_c), w2_ref, b2_ref, wbr_ref, wout_ref,
                           og_ref[...], ob_ref[...], alpha)


def _main_call(alpha, x, h, kv, kx, vx, mod, p, tabs):
    bsz, seq, d = x.shape
    w = W_BRANCH
    tl = MAIN_ROWS * GRID_W
    half = tl // 2
    nt = seq // tl
    n_half = seq // half
    lc = kx.shape[1]

    consts = [mod, p["w2"], p["b2"], p["sgu_g"], p["sgu_b"], p["ws"], p["bs_tab"], p["wbr"], p["wout"],
              p["ln_g"], p["ln_b"]]
    def window_specs(j):
        return [pl.BlockSpec((1, 1, half, w), lambda b, i: (j, b, jnp.maximum(2 * i - 1, 0), 0)),
                pl.BlockSpec((1, 1, tl, w), lambda b, i: (j, b, i, 0)),
                pl.BlockSpec((1, 1, half, w), lambda b, i: (j, b, jnp.minimum(2 * i + 2, n_half - 1), 0))]

    ctx_spec = pl.BlockSpec((1, lc, w), lambda b, i: (b, 0, 0))
    tab_spec = pl.BlockSpec((tl, LANES), lambda b, i: (i, 0))
    in_specs = ([pl.BlockSpec((1, tl, d), lambda b, i: (b, i, 0)),
                 pl.BlockSpec((2, 1, tl, w), lambda b, i: (0, b, i, 0))]
                + window_specs(0) + window_specs(1) + [ctx_spec, ctx_spec]
                + [_const_spec(a.shape) for a in consts]
                + [tab_spec, tab_spec, tab_spec, _const_spec(tabs["bias"].shape)])
    scratch = [pltpu.VMEM((2 * tl, w), BF16), pltpu.VMEM((2 * tl, w), BF16),
               pltpu.VMEM((tl, w), BF16), pltpu.VMEM((tl, w), F32)]
    return pl.pallas_call(
        functools.partial(_main_kernel, alpha),
        grid=(bsz, nt),
        in_specs=in_specs,
        out_specs=pl.BlockSpec((1, tl, d), lambda b, i: (b, i, 0)),
        out_shape=jax.ShapeDtypeStruct((bsz, seq, d), F32),
        scratch_shapes=scratch,
        compiler_params=pltpu.CompilerParams(dimension_semantics=("arbitrary", "arbitrary"),
                                             vmem_limit_bytes=VMEM_LIMIT),
        name="latent_main",
    )(x, h, kv, kv, kv, kv, kv, kv, kx, vx, *consts, tabs["cos"], tabs["sin_lo"], tabs["sin_hi"], tabs["bias"])


def _block_diag_groups(wg):
    per = GATE_GROUP // B_BW
    g = B_BLOCKS // per
    wg = wg.reshape(2, g, per, B_BW, B_BW)
    eye = jnp.eye(per, dtype=wg.dtype)
    return jnp.einsum("dghij,hk->dghikj", wg, eye).reshape(2, g, GATE_GROUP, GATE_GROUP)


def _rope_tables(seq):
    pos = jnp.arange(seq)
    rows, cols = (pos // GRID_W).astype(F32), (pos % GRID_W).astype(F32)
    quarter = C_HD // 4
    inv_freq = ROPE_BASE ** (-jnp.arange(quarter, dtype=F32) / quarter)
    lane = np.arange(LANES) % C_HD
    use_row = lane < C_HD // 2
    first = (lane % (C_HD // 2)) < quarter
    ang_r = rows[:, None] * inv_freq
    ang_c = cols[:, None] * inv_freq
    idx = lane % quarter
    ang = jnp.where(use_row[None, :], ang_r[:, idx], ang_c[:, idx])
    sin = jnp.sin(ang)
    tabs = {"cos": jnp.cos(ang),
            "sin_lo": jnp.where(first[None, :], -sin, 0.0),
            "sin_hi": jnp.where(first[None, :], 0.0, sin)}
    tabs["cos2"] = jnp.stack([tabs["cos"], jnp.ones_like(sin)])
    tabs["sin_lo2"] = jnp.stack([tabs["sin_lo"], jnp.zeros_like(sin)])
    tabs["sin_hi2"] = jnp.stack([tabs["sin_hi"], jnp.zeros_like(sin)])
    return tabs


def _bias_table(rpb_l):
    qc = np.arange(GRID_W)
    c0 = np.clip(qc - WIN_C // 2, 0, GRID_W - WIN_C)
    kc = np.arange(GRID_W)
    allowed = (kc[None, :] >= c0[:, None]) & (kc[None, :] < c0[:, None] + WIN_C)
    cidx = np.clip(kc[None, :] - qc[:, None] + (WIN_C - 1), 0, 2 * WIN_C - 2)
    ridx = np.arange(WIN_R)[None, :] - np.arange(WIN_R)[:, None] + (WIN_R - 1)
    t = rpb_l[:, ridx[:, :, None, None], cidx[None, None, :, :]]
    t = jnp.where(allowed[None, None, None], t, NEG_BIG)
    return t.transpose(1, 0, 3, 2, 4).reshape(WIN_R, rpb_l.shape[0], GRID_W, WIN_R * GRID_W)


def _layer_params(l, w_in, b_in, sgu_ln_g, sgu_ln_b, w_s, b_s, conv_w, conv_b, lru_wa, lru_ba, lru_wx, lru_bx,
                  lru_lam, w_br, w_out, ln_g, ln_b):
    w = W_BRANCH
    wi, bi = w_in[l], b_in[l]
    sel1 = np.r_[3 * w:4 * w, 6 * w:8 * w]
    sel2 = np.r_[0:3 * w, 4 * w:6 * w, 8 * w:wi.shape[1]]
    gw = jnp.concatenate([_block_diag_groups(lru_wa[l]), _block_diag_groups(lru_wx[l])], axis=-1)
    return {
        "w1": wi[:, sel1].astype(BF16), "b1": bi[sel1][None, :],
        "wbx": wi[:, 3 * w:4 * w].astype(BF16), "bbx": bi[3 * w:4 * w][None, :],
        "wkv": jnp.stack([wi[:, 6 * w:7 * w], wi[:, 7 * w:8 * w]]).astype(BF16),
        "bkv": jnp.stack([bi[6 * w:7 * w], bi[7 * w:8 * w]])[:, None, :],
        "w2": wi[:, sel2].astype(BF16), "b2": bi[sel2][None, :],
        "conv_w": conv_w[l], "conv_b": conv_b[l][None, :],
        "gw": gw.astype(BF16), "gba": lru_ba[l][:, None, :], "gbx": lru_bx[l][:, None, :],
        "lam": lru_lam[l][:, None, :],
        "sgu_g": sgu_ln_g[l][None, :], "sgu_b": sgu_ln_b[l][None, :],
        "ws": w_s[l].astype(BF16), "bs_tab": jnp.repeat(b_s[l].T, LANES, axis=1),
        "wbr": w_br[l].astype(BF16), "wout": w_out[l].astype(BF16),
        "ln_g": ln_g[l][None, :], "ln_b": ln_b[l][None, :],
    }


def kernel(x, c, ctx, c_ctx, w_ada, b_ada, w_in, b_in, sgu_ln_g, sgu_ln_b, w_s, b_s, conv_w, conv_b, lru_wa,
           lru_ba, lru_wx, lru_bx, lru_lam, rpb, w_br, w_out, ln_g, ln_b):
    bsz, seq, d = x.shape
    depth = w_ada.shape[0]
    assert d % LANES == 0 and seq % (MAIN_ROWS * GRID_W) == 0 and seq // GRID_W >= 2 * MAIN_ROWS
    assert seq % SCAN_TILE == 0 and ctx.shape[1] % CHUNK == 0 and bsz <= 8
    alpha = (2 * depth) ** 0.25

    pad = jnp.zeros((16 - bsz - 1, d), F32)
    mods = _ada_call(jnp.concatenate([c, c_ctx[None, :], pad], axis=0), w_ada, b_ada)
    tabs = _rope_tables(seq)

    xc = ctx
    for l in range(depth):
        with_ctx = l < depth - 1
        p = _layer_params(l, w_in, b_in, sgu_ln_g, sgu_ln_b, w_s, b_s, conv_w, conv_b, lru_wa, lru_ba, lru_wx,
                          lru_bx, lru_lam, w_br, w_out, ln_g, ln_b)
        ctx_out = _ctx_call(with_ctx, alpha, xc, mods[l], p)
        kx, vx, hfin = ctx_out[0], ctx_out[1], ctx_out[2]
        h0 = hfin.transpose(1, 0, 2)
        h, kv = _scan_call(x, mods[l], h0, p, tabs)
        layer_tabs = dict(tabs, bias=_bias_table(rpb[l]))
        x = _main_call(alpha, x, h, kv, kx, vx, mods[l], p, layer_tabs)
        if with_ctx:
            xc = ctx_out[3]
    return x
```

```python
import functools

import numpy as np
import jax
import jax.numpy as jnp
from jax import lax
from jax.experimental import pallas as pl
from jax.experimental.pallas import tpu as pltpu

F32 = jnp.float32
BF16 = jnp.bfloat16

GRID_W = 64
W_BRANCH = 512
N_BRANCH = 3
CHUNK = 128
A_GROUPS = 4
B_BLOCKS = 8
B_BW = W_BRANCH // B_BLOCKS
CONV_W = 4
CONV_PAD_L = 2
LRU_C = 8.0
C_HEADS = 8
C_HD = W_BRANCH // C_HEADS
WIN_R = 8
WIN_C = 16
ROPE_BASE = 10000.0
LN_EPS = 1e-5
NEG_BIG = -1e30

LANES = 128
HALO = 8
GATE_GROUP = 256
SCAN_TILE = 128
MAIN_ROWS = 8
VMEM_LIMIT = 60 * 1024 * 1024

_OFF_AU, _OFF_AV, _OFF_AG, _OFF_BG, _OFF_CQ, _OFF_CG, _OFF_GM = 0, 512, 1024, 1536, 2048, 2560, 3072


def _dotf(a, b):
    return jnp.dot(a, b, preferred_element_type=F32)


def _dot_nt(a, b):
    return lax.dot_general(a, b, (((1,), (1,)), ((), ())), preferred_element_type=F32)


def _sigmoid(x):
    return 1.0 / (1.0 + jnp.exp(-x))


def _silu(x):
    return x * _sigmoid(x)


def _gelu(x):
    return 0.5 * x * (1.0 + lax.erf(x * 0.7071067811865476))


def _softplus(x):
    return jnp.maximum(x, 0.0) + jnp.log1p(jnp.exp(-jnp.abs(x)))


def _ln(x, g, b):
    mu = jnp.mean(x, axis=-1, keepdims=True)
    xc = x - mu
    var = jnp.mean(xc * xc, axis=-1, keepdims=True)
    return xc * lax.rsqrt(var + LN_EPS) * g + b


def _rope(x, cos, sin_lo, sin_hi):
    outs = []
    for s in range(x.shape[1] // LANES):
        xs = x[:, s * LANES:(s + 1) * LANES]
        outs.append(xs * cos + pltpu.roll(xs, LANES - 16, 1) * sin_lo + pltpu.roll(xs, 16, 1) * sin_hi)
    return jnp.concatenate(outs, axis=1)


def _conv_from_ext(ext_ref, n, cw, cb):
    acc = cb
    for j in range(CONV_W):
        acc = acc + cw[j:j + 1, :] * ext_ref[pl.ds(HALO - CONV_PAD_L + j, n), :]
    return acc


def _lru_coeffs(xl, gw0, gw1, ba, bx, lam):
    xb = xl.astype(BF16)
    o0 = _dotf(xb[:, :GATE_GROUP], gw0)
    o1 = _dotf(xb[:, GATE_GROUP:], gw1)
    r = _sigmoid(jnp.concatenate([o0[:, :GATE_GROUP], o1[:, :GATE_GROUP]], axis=1) + ba)
    i = _sigmoid(jnp.concatenate([o0[:, GATE_GROUP:], o1[:, GATE_GROUP:]], axis=1) + bx)
    log_a = (-LRU_C) * r * _softplus(-lam)
    a = jnp.exp(log_a)
    t = jnp.tanh(log_a)
    one_minus_a2 = (-2.0 * t) / (1.0 - t)
    return a, jnp.sqrt(one_minus_a2) * (i * xl)


def _branch_a(zu, zv, zg, lng, lnb, ws_ref, bs_tab):
    gu = _gelu(zu)
    gv = _ln(_gelu(zv), lng, lnb).astype(BF16)
    rows = []
    for n in range(zu.shape[0] // CHUNK):
        cols = [_dotf(ws_ref[g], gv[n * CHUNK:(n + 1) * CHUNK, g * LANES:(g + 1) * LANES])
                for g in range(A_GROUPS)]
        rows.append(jnp.concatenate(cols, axis=1) + bs_tab)
    return gu * jnp.concatenate(rows, axis=0) * _silu(zg)


def _head_mask(rows, second):
    lane = lax.broadcasted_iota(jnp.int32, (rows, LANES), 1)
    return (lane >= C_HD) if second else (lane < C_HD)


def _ada_kernel(cc_ref, w_ref, b_ref, o_ref):
    s = _silu(cc_ref[...])
    o_ref[0] = jnp.dot(s, w_ref[0], preferred_element_type=F32, precision=lax.Precision.HIGHEST) + b_ref[0]


def _ada_call(cc, w_ada, b_ada):
    depth, d, n3 = w_ada.shape
    nb = n3 // d
    return pl.pallas_call(
        _ada_kernel,
        grid=(depth, nb),
        in_specs=[pl.BlockSpec(cc.shape, lambda l, j: (0, 0)),
                  pl.BlockSpec((1, d, d), lambda l, j: (l, 0, j)),
                  pl.BlockSpec((1, 1, d), lambda l, j: (l, 0, j))],
        out_specs=pl.BlockSpec((1, cc.shape[0], d), lambda l, j: (l, 0, j)),
        out_shape=jax.ShapeDtypeStruct((depth, cc.shape[0], n3), F32),
        compiler_params=pltpu.CompilerParams(dimension_semantics=("arbitrary", "arbitrary"),
                                             vmem_limit_bytes=VMEM_LIMIT),
        name="adaln_modulation",
    )(cc, w_ada, b_ada.reshape(depth, 1, n3))


def _merge_tail(x, u, gate, ys, w2_ref, b2_ref, wbr_ref, wout_ref, og, ob, alpha):
    m = None
    for n in range(N_BRANCH):
        c0 = _OFF_GM + n * x.shape[1]
        g = _sigmoid(_dotf(u, w2_ref[:, c0:c0 + x.shape[1]]) + b2_ref[:, c0:c0 + x.shape[1]])
        t = g * _dotf(ys[n].astype(BF16), wbr_ref[n])
        m = t if m is None else m + t
    return _ln(alpha * x + gate * _dotf(m.astype(BF16), wout_ref[...]), og, ob)


def _ctx_kernel(with_ctx, alpha, n_batch, *refs):
    (xc_ref, mod_ref, w1_ref, b1_ref, cw_ref, cb_ref, gw_ref, gba_ref, gbx_ref, lam_ref) = refs[:10]
    if with_ctx:
        (w2_ref, b2_ref, lng_ref, lnb_ref, ws_ref, bs_ref, wbr_ref, wout_ref, og_ref, ob_ref) = refs[10:20]
        k_ref, v_ref, hfin_ref, xo_ref = refs[20:24]
        ext_scr, af_scr, bf_scr, ar_scr, br_scr, hf_scr, hr_scr = refs[24:]
    else:
        k_ref, v_ref, hfin_ref = refs[10:13]
        ext_scr, af_scr, bf_scr, ar_scr, br_scr, hf_scr, hr_scr = refs[13:]
    lc, d = xc_ref.shape[1], xc_ref.shape[2]
    w = W_BRANCH

    xc = xc_ref[0]
    mod = mod_ref[n_batch:n_batch + 1, :]
    shift, scale, gate = mod[:, 0:d], mod[:, d:2 * d], mod[:, 2 * d:3 * d]
    u = (xc * (1.0 + scale) + shift).astype(BF16)

    z1 = _dotf(u, w1_ref[...]) + b1_ref[...]
    bx = z1[:, 0:w]
    kc = z1[:, w:2 * w].astype(BF16)
    vc = z1[:, 2 * w:3 * w].astype(BF16)
    k_ref[0] = kc
    v_ref[0] = vc

    zeros = jnp.zeros((HALO, w), F32)
    ext_scr[pl.ds(0, HALO), :] = zeros
    ext_scr[pl.ds(HALO, lc), :] = bx
    ext_scr[pl.ds(HALO + lc, HALO), :] = zeros
    xl = _conv_from_ext(ext_scr, lc, cw_ref[...], cb_ref[...])

    a_f, b_f = _lru_coeffs(xl, gw_ref[0, 0], gw_ref[0, 1], gba_ref[0], gbx_ref[0], lam_ref[0])
    a_r, b_r = _lru_coeffs(xl, gw_ref[1, 0], gw_ref[1, 1], gba_ref[1], gbx_ref[1], lam_ref[1])
    af_scr[...] = a_f
    bf_scr[...] = b_f
    ar_scr[...] = a_r
    br_scr[...] = b_r

    def body(t, carry):
        hf, hr = carry
        tr = lc - 1 - t
        hf = af_scr[pl.ds(t, 1), :] * hf + bf_scr[pl.ds(t, 1), :]
        hr = ar_scr[pl.ds(tr, 1), :] * hr + br_scr[pl.ds(tr, 1), :]
        hf_scr[pl.ds(t, 1), :] = hf
        hr_scr[pl.ds(tr, 1), :] = hr
        return hf, hr

    h0 = jnp.zeros((1, w), F32)
    hf, hr = lax.fori_loop(0, lc, body, (h0, h0))
    hfin_ref[0, 0:1, :] = hf
    hfin_ref[0, 1:2, :] = hr

    if not with_ctx:
        return

    def proj(c0, n):
        return _dotf(u, w2_ref[:, c0:c0 + n]) + b2_ref[:, c0:c0 + n]

    y_a = _branch_a(proj(_OFF_AU, w), proj(_OFF_AV, w), proj(_OFF_AG, w),
                    lng_ref[...], lnb_ref[...], ws_ref, bs_ref[...])
    y_b = (hf_scr[...] + hr_scr[...]) * _silu(proj(_OFF_BG, w))

    q = (proj(_OFF_CQ, w) * (C_HD ** -0.5)).astype(BF16)
    outs = []
    for p in range(w // LANES):
        qp = q[:, p * LANES:(p + 1) * LANES]
        kp = kc[:, p * LANES:(p + 1) * LANES]
        vp = vc[:, p * LANES:(p + 1) * LANES]
        acc = None
        for hh in range(2):
            msk = _head_mask(lc, hh == 1)
            s = _dot_nt(jnp.where(msk, qp, jnp.zeros_like(qp)), kp)
            e = jnp.exp(s - jnp.max(s, axis=-1, keepdims=True))
            o = _dotf(e.astype(BF16), vp) / jnp.sum(e, axis=-1, keepdims=True)
            acc = o if acc is None else jnp.where(msk, o, acc)
        outs.append(acc)
    y_c = jnp.concatenate(outs, axis=1) * _silu(proj(_OFF_CG, w))

    xo_ref[0] = _merge_tail(xc, u, gate, (y_a, y_b, y_c), w2_ref, b2_ref, wbr_ref, wout_ref,
                            og_ref[...], ob_ref[...], alpha)


def _const_spec(shape):
    nd = len(shape)
    return pl.BlockSpec(shape, lambda *_: (0,) * nd, pipeline_mode=pl.Buffered(1))


def _ctx_call(with_ctx, alpha, xc, mod, p):
    bsz, lc, d = xc.shape
    w = W_BRANCH
    ins = [xc, mod, p["w1"], p["b1"], p["conv_w"], p["conv_b"], p["gw"], p["gba"], p["gbx"], p["lam"]]
    in_specs = [pl.BlockSpec((1, lc, d), lambda b: (b, 0, 0))] + [_const_spec(a.shape) for a in ins[1:]]
    out_shape = [jax.ShapeDtypeStruct((bsz, lc, w), BF16), jax.ShapeDtypeStruct((bsz, lc, w), BF16),
                 jax.ShapeDtypeStruct((bsz, 2, w), F32)]
    out_specs = [pl.BlockSpec((1, lc, w), lambda b: (b, 0, 0)), pl.BlockSpec((1, lc, w), lambda b: (b, 0, 0)),
                 pl.BlockSpec((1, 2, w), lambda b: (b, 0, 0))]
    if with_ctx:
        extra = [p["w2"], p["b2"], p["sgu_g"], p["sgu_b"], p["ws"], p["bs_tab"], p["wbr"], p["wout"],
                 p["ln_g"], p["ln_b"]]
        ins += extra
        in_specs += [_const_spec(a.shape) for a in extra]
        out_shape.append(jax.ShapeDtypeStruct((bsz, lc, d), F32))
        out_specs.append(pl.BlockSpec((1, lc, d), lambda b: (b, 0, 0)))
    scratch = [pltpu.VMEM((lc + 2 * HALO, w), F32)] + [pltpu.VMEM((lc, w), F32) for _ in range(6)]
    return pl.pallas_call(
        functools.partial(_ctx_kernel, with_ctx, alpha, bsz),
        grid=(bsz,),
        in_specs=in_specs,
        out_specs=out_specs,
        out_shape=out_shape,
        scratch_shapes=scratch,
        compiler_params=pltpu.CompilerParams(dimension_semantics=("arbitrary",), vmem_limit_bytes=VMEM_LIMIT),
        name="context_layer" if with_ctx else "context_kv_state",
    )(*ins)


def _scan_kernel(x_ref, xp_ref, xn_ref, mod_ref, w1_ref, b1_ref, wkv_ref, bkv_ref, cw_ref, cb_ref, gw_ref, gba_ref,
                 gbx_ref, lam_ref, h0_ref, cos_ref, slo_ref, shi_ref,
                 h_ref, kv_ref,
                 a_scr, b_scr, ext_scr, hst_scr):
    phase = pl.program_id(0)
    i = pl.program_id(1)
    nt = pl.num_programs(1)
    tile = jnp.where(phase == 0, i, nt - 1 - i)
    nb, tl, d = x_ref.shape
    w = W_BRANCH

    def modulate(xv, b):
        return (xv * (1.0 + mod_ref[b:b + 1, d:2 * d]) + mod_ref[b:b + 1, 0:d]).astype(BF16)

    u = jnp.concatenate([modulate(x_ref[b], b) for b in range(nb)], axis=0)
    u_halo = jnp.concatenate([modulate(xp_ref[b], b) for b in range(nb)]
                             + [modulate(xn_ref[b], b) for b in range(nb)], axis=0)

    bx = _dotf(u, w1_ref[...]) + b1_ref[...]
    bx_halo = _dotf(u_halo, w1_ref[...]) + b1_ref[...]
    bx_prev = jnp.where(tile == 0, 0.0, bx_halo[0:nb * HALO])
    bx_next = jnp.where(tile == nt - 1, 0.0, bx_halo[nb * HALO:2 * nb * HALO])

    kv = _dotf(u, wkv_ref[0]) + bkv_ref[0]
    cos, slo, shi = cos_ref[0], slo_ref[0], shi_ref[0]
    for b in range(nb):
        kv_ref[0, b] = _rope(kv[b * tl:(b + 1) * tl], cos, slo, shi).astype(BF16)

    cw, cb = cw_ref[...], cb_ref[...]
    xls = []
    for b in range(nb):
        ext_scr[b, pl.ds(0, HALO), :] = bx_prev[b * HALO:(b + 1) * HALO]
        ext_scr[b, pl.ds(HALO, tl), :] = bx[b * tl:(b + 1) * tl]
        ext_scr[b, pl.ds(HALO + tl, HALO), :] = bx_next[b * HALO:(b + 1) * HALO]
        xls.append(_conv_from_ext(ext_scr.at[b], tl, cw, cb))
    xl = jnp.concatenate(xls, axis=0)

    a, bco = _lru_coeffs(xl, gw_ref[0, 0], gw_ref[0, 1], gba_ref[0], gbx_ref[0], lam_ref[0])
    a_scr[...] = a
    b_scr[...] = bco

    @pl.when(i == 0)
    def _():
        hst_scr[...] = h0_ref[0]

    def body(t, hs):
        tt = jnp.where(phase == 0, t, tl - 1 - t)
        new = []
        for b in range(nb):
            row = b * tl + tt
            h = a_scr[pl.ds(row, 1), :] * hs[b] + b_scr[pl.ds(row, 1), :]
            h_ref[0, b, pl.ds(tt, 1), :] = h
            new.append(h)
        return tuple(new)

    hs = lax.fori_loop(0, tl, body, tuple(hst_scr[pl.ds(b, 1), :] for b in range(nb)))
    for b in range(nb):
        hst_scr[pl.ds(b, 1), :] = hs[b]


def _scan_call(x, mod, h0, p, tabs):
    bsz, seq, d = x.shape
    w = W_BRANCH
    tl = SCAN_TILE
    nt = seq // tl
    hb = tl // HALO
    n_halo = seq // HALO

    def tile_of(ph, i):
        return jnp.where(ph == 0, i, nt - 1 - i)

    def dir_spec(a):
        nd = a.ndim
        return pl.BlockSpec((1,) + a.shape[1:], lambda ph, i: (ph,) + (0,) * (nd - 1))

    in_specs = [
        pl.BlockSpec((bsz, tl, d), lambda ph, i: (0, tile_of(ph, i), 0)),
        pl.BlockSpec((bsz, HALO, d), lambda ph, i: (0, jnp.maximum(tile_of(ph, i) * hb - 1, 0), 0)),
        pl.BlockSpec((bsz, HALO, d), lambda ph, i: (0, jnp.minimum((tile_of(ph, i) + 1) * hb, n_halo - 1), 0)),
        _const_spec(mod.shape), _const_spec(p["wbx"].shape), _const_spec(p["bbx"].shape),
        dir_spec(p["wkv"]), dir_spec(p["bkv"]),
        _const_spec(p["conv_w"].shape), _const_spec(p["conv_b"].shape),
        dir_spec(p["gw"]), dir_spec(p["gba"]), dir_spec(p["gbx"]), dir_spec(p["lam"]), dir_spec(h0),
        pl.BlockSpec((1, tl, LANES), lambda ph, i: (ph, tile_of(ph, i), 0)),
        pl.BlockSpec((1, tl, LANES), lambda ph, i: (ph, tile_of(ph, i), 0)),
        pl.BlockSpec((1, tl, LANES), lambda ph, i: (ph, tile_of(ph, i), 0)),
    ]
    out_specs = [pl.BlockSpec((1, bsz, tl, w), lambda ph, i: (ph, 0, tile_of(ph, i), 0)),
                 pl.BlockSpec((1, bsz, tl, w), lambda ph, i: (ph, 0, tile_of(ph, i), 0))]
    out_shape = [jax.ShapeDtypeStruct((2, bsz, seq, w), F32), jax.ShapeDtypeStruct((2, bsz, seq, w), BF16)]
    scratch = [pltpu.VMEM((bsz * tl, w), F32), pltpu.VMEM((bsz * tl, w), F32),
               pltpu.VMEM((bsz, tl + 2 * HALO, w), F32), pltpu.VMEM((bsz, w), F32)]
    return pl.pallas_call(
        _scan_kernel,
        grid=(2, nt),
        in_specs=in_specs,
        out_specs=out_specs,
        out_shape=out_shape,
        scratch_shapes=scratch,
        compiler_params=pltpu.CompilerParams(dimension_semantics=("arbitrary", "arbitrary"),
                                             vmem_limit_bytes=VMEM_LIMIT),
        name="latent_scan",
    )(x, x, x, mod, p["wbx"], p["bbx"], p["wkv"], p["bkv"], p["conv_w"], p["conv_b"], p["gw"], p["gba"], p["gbx"],
      p["lam"], h0, tabs["cos2"], tabs["sin_lo2"], tabs["sin_hi2"])


def _main_kernel(alpha, x_ref, h_ref, kp_ref, kc_ref, kn_ref, vp_ref, vc_ref, vn_ref, kx_ref, vx_ref, mod_ref,
                 w2_ref, b2_ref, lng_ref, lnb_ref, ws_ref, bs_ref, wbr_ref, wout_ref, og_ref, ob_ref,
                 cos_ref, slo_ref, shi_ref, bm_ref, rm_ref,
                 o_ref,
                 kbuf, vbuf, q_scr, yc_scr, kx_scr, vx_scr):
    b = pl.program_id(0)
    i = pl.program_id(1)
    nt = pl.num_programs(1)
    tl, d = x_ref.shape[1], x_ref.shape[2]
    w = W_BRANCH
    half = kp_ref.shape[2]
    n_rows = tl // GRID_W

    x = x_ref[0]
    mod = mod_ref[pl.ds(b, 1), :]
    shift, scale, gate = mod[:, 0:d], mod[:, d:2 * d], mod[:, 2 * d:3 * d]
    u = (x * (1.0 + scale) + shift).astype(BF16)

    def proj(c0, n):
        return _dotf(u, w2_ref[:, c0:c0 + n]) + b2_ref[:, c0:c0 + n]

    y_a = _branch_a(proj(_OFF_AU, w), proj(_OFF_AV, w), proj(_OFF_AG, w),
                    lng_ref[...], lnb_ref[...], ws_ref, bs_ref[...])
    y_b = (h_ref[0, 0] + h_ref[1, 0]) * _silu(proj(_OFF_BG, w))

    q = (_rope(proj(_OFF_CQ, w), cos_ref[...], slo_ref[...], shi_ref[...]) * (C_HD ** -0.5)).astype(BF16)
    n_pairs = w // LANES
    for p in range(n_pairs):
        ls = slice(p * LANES, (p + 1) * LANES)
        q_scr[p] = q[:, ls]
        kbuf[p, pl.ds(0, half), :] = kp_ref[0, 0, :, ls]
        kbuf[p, pl.ds(half, tl), :] = kc_ref[0, 0, :, ls]
        kbuf[p, pl.ds(half + tl, half), :] = kn_ref[0, 0, :, ls]
        vbuf[p, pl.ds(0, half), :] = vp_ref[0, 0, :, ls]
        vbuf[p, pl.ds(half, tl), :] = vc_ref[0, 0, :, ls]
        vbuf[p, pl.ds(half + tl, half), :] = vn_ref[0, 0, :, ls]
        kx_scr[p] = kx_ref[0, :, ls]
        vx_scr[p] = vx_ref[0, :, ls]

    hrows = n_rows // 2
    wrows = hrows + WIN_R
    total_rows = nt * n_rows
    codes = []
    for hf in range(2):
        per_rq = []
        for rq in range(hrows):
            r = i * n_rows + hf * hrows + rq
            w_lo = jnp.clip(r - WIN_R // 2, 0, total_rows - WIN_R) - (r - WIN_R // 2) + rq
            per_j = []
            for j in range(wrows // 2):
                va = ((2 * j >= w_lo) & (2 * j < w_lo + WIN_R)).astype(jnp.int32)
                vb = ((2 * j + 1 >= w_lo) & (2 * j + 1 < w_lo + WIN_R)).astype(jnp.int32)
                per_j.append(2 * va + vb)
            per_rq.append(per_j)
        codes.append(per_rq)

    nq = hrows * GRID_W
    nk = wrows * GRID_W

    def pair_body(p, carry):
        kx = kx_scr[p]
        vx = vx_scr[p]
        for hf in range(2):
            qp = q_scr[p, pl.ds(hf * nq, nq), :]
            kw = kbuf[p, pl.ds(hf * nq, nk), :]
            vw = vbuf[p, pl.ds(hf * nq, nk), :]
            acc = None
            for hh in range(2):
                msk = _head_mask(nq, hh == 1)
                qm = jnp.where(msk, qp, jnp.zeros_like(qp))
                s_loc = _dot_nt(qm, kw)
                s_ctx = _dot_nt(qm, kx)
                blocks = []
                for rq in range(hrows):
                    row = []
                    for j in range(wrows // 2):
                        blk = s_loc[rq * GRID_W:(rq + 1) * GRID_W, j * LANES:(j + 1) * LANES]
                        bias = bm_ref[2 * p + hh, 2 * j - rq + hrows - 1] + rm_ref[codes[hf][rq][j]]
                        row.append(blk + bias)
                    blocks.append(jnp.concatenate(row, axis=1))
                s_loc = jnp.concatenate(blocks, axis=0)
                mx = jnp.maximum(jnp.max(s_loc, axis=-1, keepdims=True), jnp.max(s_ctx, axis=-1, keepdims=True))
                e_loc = jnp.exp(s_loc - mx)
                e_ctx = jnp.exp(s_ctx - mx)
                den = jnp.sum(e_loc, axis=-1, keepdims=True) + jnp.sum(e_ctx, axis=-1, keepdims=True)
                o = (_dotf(e_loc.astype(BF16), vw) + _dotf(e_ctx.astype(BF16), vx)) / den
                acc = o if acc is None else jnp.where(msk, o, acc)
            yc_scr[p, pl.ds(hf * nq, nq), :] = acc
        return carry

    lax.fori_loop(0, n_pairs, pair_body, 0)
    y_c = jnp.concatenate([yc_scr[p] for p in range(n_pairs)], axis=1) * _silu(proj(_OFF_CG, w))

    o_ref[0] = _merge_tail(x, u, gate, (y_a, y_b, y_c), w2_ref, b2_ref, wbr_ref, wout_ref,
                           og_ref[...], ob_ref[...], alpha)


def _main_call(alpha, x, h, kv, kx, vx, mod, p, tabs):
    bsz, seq, d = x.shape
    w = W_BRANCH
    tl = MAIN_ROWS * GRID_W
    half = tl // 2
    nt = seq // tl
    n_half = seq // half
    lc = kx.shape[1]

    consts = [mod, p["w2"], p["b2"], p["sgu_g"], p["sgu_b"], p["ws"], p["bs_tab"], p["wbr"], p["wout"],
              p["ln_g"], p["ln_b"]]
    def window_specs(j):
        return [pl.BlockSpec((1, 1, half, w), lambda b, i: (j, b, jnp.maximum(2 * i - 1, 0), 0)),
                pl.BlockSpec((1, 1, tl, w), lambda b, i: (j, b, i, 0)),
                pl.BlockSpec((1, 1, half, w), lambda b, i: (j, b, jnp.minimum(2 * i + 2, n_half - 1), 0))]

    ctx_spec = pl.BlockSpec((1, lc, w), lambda b, i: (b, 0, 0))
    tab_spec = pl.BlockSpec((tl, LANES), lambda b, i: (i, 0))
    in_specs = ([pl.BlockSpec((1, tl, d), lambda b, i: (b, i, 0)),
                 pl.BlockSpec((2, 1, tl, w), lambda b, i: (0, b, i, 0))]
                + window_specs(0) + window_specs(1) + [ctx_spec, ctx_spec]
                + [_const_spec(a.shape) for a in consts]
                + [tab_spec, tab_spec, tab_spec, _const_spec(tabs["bias"].shape),
                   _const_spec(tabs["row_mask"].shape)])
    n_pairs = w // LANES
    scratch = [pltpu.VMEM((n_pairs, 2 * tl, LANES), BF16), pltpu.VMEM((n_pairs, 2 * tl, LANES), BF16),
               pltpu.VMEM((n_pairs, tl, LANES), BF16), pltpu.VMEM((n_pairs, tl, LANES), F32),
               pltpu.VMEM((n_pairs, lc, LANES), BF16), pltpu.VMEM((n_pairs, lc, LANES), BF16)]
    return pl.pallas_call(
        functools.partial(_main_kernel, alpha),
        grid=(bsz, nt),
        in_specs=in_specs,
        out_specs=pl.BlockSpec((1, tl, d), lambda b, i: (b, i, 0)),
        out_shape=jax.ShapeDtypeStruct((bsz, seq, d), F32),
        scratch_shapes=scratch,
        compiler_params=pltpu.CompilerParams(dimension_semantics=("arbitrary", "arbitrary"),
                                             vmem_limit_bytes=VMEM_LIMIT),
        name="latent_main",
    )(x, h, kv, kv, kv, kv, kv, kv, kx, vx, *consts, tabs["cos"], tabs["sin_lo"], tabs["sin_hi"], tabs["bias"],
      tabs["row_mask"])


def _block_diag_groups(wg):
    per = GATE_GROUP // B_BW
    g = B_BLOCKS // per
    wg = wg.reshape(2, g, per, B_BW, B_BW)
    eye = jnp.eye(per, dtype=wg.dtype)
    return jnp.einsum("dghij,hk->dghikj", wg, eye).reshape(2, g, GATE_GROUP, GATE_GROUP)


def _rope_tables(seq):
    pos = jnp.arange(seq)
    rows, cols = (pos // GRID_W).astype(F32), (pos % GRID_W).astype(F32)
    quarter = C_HD // 4
    inv_freq = ROPE_BASE ** (-jnp.arange(quarter, dtype=F32) / quarter)
    lane = np.arange(LANES) % C_HD
    use_row = lane < C_HD // 2
    first = (lane % (C_HD // 2)) < quarter
    inv_lane = jnp.concatenate([inv_freq] * (LANES // quarter))
    ang = jnp.where(use_row[None, :], rows[:, None], cols[:, None]) * inv_lane[None, :]
    sin = jnp.sin(ang)
    tabs = {"cos": jnp.cos(ang),
            "sin_lo": jnp.where(first[None, :], -sin, 0.0),
            "sin_hi": jnp.where(first[None, :], 0.0, sin)}
    tabs["cos2"] = jnp.stack([tabs["cos"], jnp.ones_like(sin)])
    tabs["sin_lo2"] = jnp.stack([tabs["sin_lo"], jnp.zeros_like(sin)])
    tabs["sin_hi2"] = jnp.stack([tabs["sin_hi"], jnp.zeros_like(sin)])
    return tabs


def _bias_table(rpb_l):
    qc = np.arange(GRID_W)
    c0 = np.clip(qc - WIN_C // 2, 0, GRID_W - WIN_C)
    kc = np.arange(GRID_W)
    allowed = (kc[None, :] >= c0[:, None]) & (kc[None, :] < c0[:, None] + WIN_C)
    cidx = np.clip(kc[None, :] - qc[:, None] + (WIN_C - 1), 0, 2 * WIN_C - 2)
    t = None
    for cc in range(2 * WIN_C - 1):
        sel = jnp.asarray((cidx == cc) & allowed, F32)
        term = rpb_l[:, :, cc][:, :, None, None] * sel[None, None]
        t = term if t is None else t + term
    t = jnp.where(allowed[None, None], t, NEG_BIG)
    return jnp.concatenate([t[:, :-1], t[:, 1:]], axis=-1)


def _row_mask_table():
    lane_first = np.arange(LANES) < GRID_W
    tab = np.zeros((4, GRID_W, LANES), np.float32)
    for a in range(2):
        for b in range(2):
            ok = np.where(lane_first, bool(a), bool(b))
            tab[2 * a + b] = np.where(ok, 0.0, NEG_BIG)[None, :]
    return jnp.asarray(tab)


def _layer_params(l, w_in, b_in, sgu_ln_g, sgu_ln_b, w_s, b_s, conv_w, conv_b, lru_wa, lru_ba, lru_wx, lru_bx,
                  lru_lam, w_br, w_out, ln_g, ln_b):
    w = W_BRANCH
    wi, bi = w_in[l], b_in[l]
    sel1 = np.r_[3 * w:4 * w, 6 * w:8 * w]
    sel2 = np.r_[0:3 * w, 4 * w:6 * w, 8 * w:wi.shape[1]]
    gw = jnp.concatenate([_block_diag_groups(lru_wa[l]), _block_diag_groups(lru_wx[l])], axis=-1)
    return {
        "w1": wi[:, sel1].astype(BF16), "b1": bi[sel1][None, :],
        "wbx": wi[:, 3 * w:4 * w].astype(BF16), "bbx": bi[3 * w:4 * w][None, :],
        "wkv": jnp.stack([wi[:, 6 * w:7 * w], wi[:, 7 * w:8 * w]]).astype(BF16),
        "bkv": jnp.stack([bi[6 * w:7 * w], bi[7 * w:8 * w]])[:, None, :],
        "w2": wi[:, sel2].astype(BF16), "b2": bi[sel2][None, :],
        "conv_w": conv_w[l], "conv_b": conv_b[l][None, :],
        "gw": gw.astype(BF16), "gba": lru_ba[l][:, None, :], "gbx": lru_bx[l][:, None, :],
        "lam": lru_lam[l][:, None, :],
        "sgu_g": sgu_ln_g[l][None, :], "sgu_b": sgu_ln_b[l][None, :],
        "ws": w_s[l].astype(BF16), "bs_tab": jnp.repeat(b_s[l].T, LANES, axis=1),
        "wbr": w_br[l].astype(BF16), "wout": w_out[l].astype(BF16),
        "ln_g": ln_g[l][None, :], "ln_b": ln_b[l][None, :],
    }


def kernel(x, c, ctx, c_ctx, w_ada, b_ada, w_in, b_in, sgu_ln_g, sgu_ln_b, w_s, b_s, conv_w, conv_b, lru_wa,
           lru_ba, lru_wx, lru_bx, lru_lam, rpb, w_br, w_out, ln_g, ln_b):
    bsz, seq, d = x.shape
    depth = w_ada.shape[0]
    assert d % LANES == 0 and seq % (MAIN_ROWS * GRID_W) == 0 and seq // GRID_W >= 2 * MAIN_ROWS
    assert seq % SCAN_TILE == 0 and ctx.shape[1] % CHUNK == 0 and bsz <= 8
    alpha = (2 * depth) ** 0.25

    pad = jnp.zeros((16 - bsz - 1, d), F32)
    mods = _ada_call(jnp.concatenate([c, c_ctx[None, :], pad], axis=0), w_ada, b_ada)
    tabs = _rope_tables(seq)

    xc = ctx
    for l in range(depth):
        with_ctx = l < depth - 1
        p = _layer_params(l, w_in, b_in, sgu_ln_g, sgu_ln_b, w_s, b_s, conv_w, conv_b, lru_wa, lru_ba, lru_wx,
                          lru_bx, lru_lam, w_br, w_out, ln_g, ln_b)
        ctx_out = _ctx_call(with_ctx, alpha, xc, mods[l], p)
        kx, vx, hfin = ctx_out[0], ctx_out[1], ctx_out[2]
        h0 = hfin.transpose(1, 0, 2)
        h, kv = _scan_call(x, mods[l], h0, p, tabs)
        layer_tabs = dict(tabs, bias=_bias_table(rpb[l]), row_mask=_row_mask_table())
        x = _main_call(alpha, x, h, kv, kx, vx, mods[l], p, layer_tabs)
        if with_ctx:
            xc = ctx_out[3]
    return x
```

```python
import functools

import numpy as np
import jax
import jax.numpy as jnp
from jax import lax
from jax.experimental import pallas as pl
from jax.experimental.pallas import tpu as pltpu

F32 = jnp.float32
BF16 = jnp.bfloat16

GRID_W = 64
W_BRANCH = 512
N_BRANCH = 3
CHUNK = 128
A_GROUPS = 4
B_BLOCKS = 8
B_BW = W_BRANCH // B_BLOCKS
CONV_W = 4
CONV_PAD_L = 2
LRU_C = 8.0
C_HEADS = 8
C_HD = W_BRANCH // C_HEADS
WIN_R = 8
WIN_C = 16
ROPE_BASE = 10000.0
LN_EPS = 1e-5
NEG_BIG = -1e30

LANES = 128
HALO = 8
GATE_GROUP = 256
SCAN_TILE = 128
MAIN_ROWS = 8
VMEM_LIMIT = 60 * 1024 * 1024

_OFF_AU, _OFF_AV, _OFF_AG, _OFF_BG, _OFF_CQ, _OFF_CG, _OFF_GM = 0, 512, 1024, 1536, 2048, 2560, 3072


def _dotf(a, b):
    return jnp.dot(a, b, preferred_element_type=F32)


def _dot_nt(a, b):
    return lax.dot_general(a, b, (((1,), (1,)), ((), ())), preferred_element_type=F32)


def _sigmoid(x):
    return 1.0 / (1.0 + jnp.exp(-x))


def _silu(x):
    return x * _sigmoid(x)


def _gelu(x):
    return 0.5 * x * (1.0 + lax.erf(x * 0.7071067811865476))


def _softplus(x):
    return jnp.maximum(x, 0.0) + jnp.log1p(jnp.exp(-jnp.abs(x)))


def _ln(x, g, b):
    mu = jnp.mean(x, axis=-1, keepdims=True)
    xc = x - mu
    var = jnp.mean(xc * xc, axis=-1, keepdims=True)
    return xc * lax.rsqrt(var + LN_EPS) * g + b


def _rope(x, cos, sin_lo, sin_hi):
    outs = []
    for s in range(x.shape[1] // LANES):
        xs = x[:, s * LANES:(s + 1) * LANES]
        outs.append(xs * cos + pltpu.roll(xs, LANES - 16, 1) * sin_lo + pltpu.roll(xs, 16, 1) * sin_hi)
    return jnp.concatenate(outs, axis=1)


def _conv_from_ext(ext_ref, n, cw, cb):
    acc = cb
    for j in range(CONV_W):
        acc = acc + cw[j:j + 1, :] * ext_ref[pl.ds(HALO - CONV_PAD_L + j, n), :]
    return acc


def _lru_coeffs(xl, gw0, gw1, ba, bx, lam):
    xb = xl.astype(BF16)
    o0 = _dotf(xb[:, :GATE_GROUP], gw0)
    o1 = _dotf(xb[:, GATE_GROUP:], gw1)
    r = _sigmoid(jnp.concatenate([o0[:, :GATE_GROUP], o1[:, :GATE_GROUP]], axis=1) + ba)
    i = _sigmoid(jnp.concatenate([o0[:, GATE_GROUP:], o1[:, GATE_GROUP:]], axis=1) + bx)
    log_a = (-LRU_C) * r * _softplus(-lam)
    a = jnp.exp(log_a)
    t = jnp.tanh(log_a)
    one_minus_a2 = (-2.0 * t) / (1.0 - t)
    return a, jnp.sqrt(one_minus_a2) * (i * xl)


def _branch_a(zu, zv, zg, lng, lnb, ws_ref, bs_tab):
    gu = _gelu(zu)
    gv = _ln(_gelu(zv), lng, lnb).astype(BF16)
    rows = []
    for n in range(zu.shape[0] // CHUNK):
        cols = [_dotf(ws_ref[g], gv[n * CHUNK:(n + 1) * CHUNK, g * LANES:(g + 1) * LANES])
                for g in range(A_GROUPS)]
        rows.append(jnp.concatenate(cols, axis=1) + bs_tab)
    return gu * jnp.concatenate(rows, axis=0) * _silu(zg)


def _head_mask(rows, second):
    lane = lax.broadcasted_iota(jnp.int32, (rows, LANES), 1)
    return (lane >= C_HD) if second else (lane < C_HD)


def _ada_kernel(cc_ref, w_ref, b_ref, o_ref):
    s = _silu(cc_ref[...])
    o_ref[0] = jnp.dot(s, w_ref[0], preferred_element_type=F32, precision=lax.Precision.HIGHEST) + b_ref[0]


def _ada_call(cc, w_ada, b_ada):
    depth, d, n3 = w_ada.shape
    nb = n3 // d
    return pl.pallas_call(
        _ada_kernel,
        grid=(depth, nb),
        in_specs=[pl.BlockSpec(cc.shape, lambda l, j: (0, 0)),
                  pl.BlockSpec((1, d, d), lambda l, j: (l, 0, j)),
                  pl.BlockSpec((1, 1, d), lambda l, j: (l, 0, j))],
        out_specs=pl.BlockSpec((1, cc.shape[0], d), lambda l, j: (l, 0, j)),
        out_shape=jax.ShapeDtypeStruct((depth, cc.shape[0], n3), F32),
        compiler_params=pltpu.CompilerParams(dimension_semantics=("arbitrary", "arbitrary"),
                                             vmem_limit_bytes=VMEM_LIMIT),
        name="adaln_modulation",
    )(cc, w_ada, b_ada.reshape(depth, 1, n3))


def _merge_tail(x, u, gate, ys, w2_ref, b2_ref, wbr_ref, wout_ref, og, ob, alpha):
    m = None
    for n in range(N_BRANCH):
        c0 = _OFF_GM + n * x.shape[1]
        g = _sigmoid(_dotf(u, w2_ref[:, c0:c0 + x.shape[1]]) + b2_ref[:, c0:c0 + x.shape[1]])
        t = g * _dotf(ys[n].astype(BF16), wbr_ref[n])
        m = t if m is None else m + t
    return _ln(alpha * x + gate * _dotf(m.astype(BF16), wout_ref[...]), og, ob)


def _ctx_kernel(with_ctx, alpha, n_batch, *refs):
    (xc_ref, mod_ref, w1_ref, b1_ref, cw_ref, cb_ref, gw_ref, gba_ref, gbx_ref, lam_ref) = refs[:10]
    if with_ctx:
        (w2_ref, b2_ref, lng_ref, lnb_ref, ws_ref, bs_ref, wbr_ref, wout_ref, og_ref, ob_ref) = refs[10:20]
        k_ref, v_ref, hfin_ref, xo_ref = refs[20:24]
        ext_scr, af_scr, bf_scr, ar_scr, br_scr, hf_scr, hr_scr = refs[24:]
    else:
        k_ref, v_ref, hfin_ref = refs[10:13]
        ext_scr, af_scr, bf_scr, ar_scr, br_scr, hf_scr, hr_scr = refs[13:]
    lc, d = xc_ref.shape[1], xc_ref.shape[2]
    w = W_BRANCH

    xc = xc_ref[0]
    mod = mod_ref[n_batch:n_batch + 1, :]
    shift, scale, gate = mod[:, 0:d], mod[:, d:2 * d], mod[:, 2 * d:3 * d]
    u = (xc * (1.0 + scale) + shift).astype(BF16)

    z1 = _dotf(u, w1_ref[...]) + b1_ref[...]
    bx = z1[:, 0:w]
    kc = z1[:, w:2 * w].astype(BF16)
    vc = z1[:, 2 * w:3 * w].astype(BF16)
    k_ref[0] = kc
    v_ref[0] = vc

    zeros = jnp.zeros((HALO, w), F32)
    ext_scr[pl.ds(0, HALO), :] = zeros
    ext_scr[pl.ds(HALO, lc), :] = bx
    ext_scr[pl.ds(HALO + lc, HALO), :] = zeros
    xl = _conv_from_ext(ext_scr, lc, cw_ref[...], cb_ref[...])

    a_f, b_f = _lru_coeffs(xl, gw_ref[0, 0], gw_ref[0, 1], gba_ref[0], gbx_ref[0], lam_ref[0])
    a_r, b_r = _lru_coeffs(xl, gw_ref[1, 0], gw_ref[1, 1], gba_ref[1], gbx_ref[1], lam_ref[1])
    af_scr[...] = a_f
    bf_scr[...] = b_f
    ar_scr[...] = a_r
    br_scr[...] = b_r

    def body(t, carry):
        hf, hr = carry
        tr = lc - 1 - t
        hf = af_scr[pl.ds(t, 1), :] * hf + bf_scr[pl.ds(t, 1), :]
        hr = ar_scr[pl.ds(tr, 1), :] * hr + br_scr[pl.ds(tr, 1), :]
        hf_scr[pl.ds(t, 1), :] = hf
        hr_scr[pl.ds(tr, 1), :] = hr
        return hf, hr

    h0 = jnp.zeros((1, w), F32)
    hf, hr = lax.fori_loop(0, lc, body, (h0, h0))
    hfin_ref[0, 0:1, :] = hf
    hfin_ref[0, 1:2, :] = hr

    if not with_ctx:
        return

    def proj(c0, n):
        return _dotf(u, w2_ref[:, c0:c0 + n]) + b2_ref[:, c0:c0 + n]

    y_a = _branch_a(proj(_OFF_AU, w), proj(_OFF_AV, w), proj(_OFF_AG, w),
                    lng_ref[...], lnb_ref[...], ws_ref, bs_ref[...])
    y_b = (hf_scr[...] + hr_scr[...]) * _silu(proj(_OFF_BG, w))

    q = (proj(_OFF_CQ, w) * (C_HD ** -0.5)).astype(BF16)
    outs = []
    for p in range(w // LANES):
        qp = q[:, p * LANES:(p + 1) * LANES]
        kp = kc[:, p * LANES:(p + 1) * LANES]
        vp = vc[:, p * LANES:(p + 1) * LANES]
        acc = None
        for hh in range(2):
            msk = _head_mask(lc, hh == 1)
            s = _dot_nt(jnp.where(msk, qp, jnp.zeros_like(qp)), kp)
            e = jnp.exp(s - jnp.max(s, axis=-1, keepdims=True))
            o = _dotf(e.astype(BF16), vp) / jnp.sum(e, axis=-1, keepdims=True)
            acc = o if acc is None else jnp.where(msk, o, acc)
        outs.append(acc)
    y_c = jnp.concatenate(outs, axis=1) * _silu(proj(_OFF_CG, w))

    xo_ref[0] = _merge_tail(xc, u, gate, (y_a, y_b, y_c), w2_ref, b2_ref, wbr_ref, wout_ref,
                            og_ref[...], ob_ref[...], alpha)


def _const_spec(shape):
    nd = len(shape)
    return pl.BlockSpec(shape, lambda *_: (0,) * nd, pipeline_mode=pl.Buffered(1))


def _ctx_call(with_ctx, alpha, xc, mod, p):
    bsz, lc, d = xc.shape
    w = W_BRANCH
    ins = [xc, mod, p["w1"], p["b1"], p["conv_w"], p["conv_b"], p["gw"], p["gba"], p["gbx"], p["lam"]]
    in_specs = [pl.BlockSpec((1, lc, d), lambda b: (b, 0, 0))] + [_const_spec(a.shape) for a in ins[1:]]
    out_shape = [jax.ShapeDtypeStruct((bsz, lc, w), BF16), jax.ShapeDtypeStruct((bsz, lc, w), BF16),
                 jax.ShapeDtypeStruct((bsz, 2, w), F32)]
    out_specs = [pl.BlockSpec((1, lc, w), lambda b: (b, 0, 0)), pl.BlockSpec((1, lc, w), lambda b: (b, 0, 0)),
                 pl.BlockSpec((1, 2, w), lambda b: (b, 0, 0))]
    if with_ctx:
        extra = [p["w2"], p["b2"], p["sgu_g"], p["sgu_b"], p["ws"], p["bs_tab"], p["wbr"], p["wout"],
                 p["ln_g"], p["ln_b"]]
        ins += extra
        in_specs += [_const_spec(a.shape) for a in extra]
        out_shape.append(jax.ShapeDtypeStruct((bsz, lc, d), F32))
        out_specs.append(pl.BlockSpec((1, lc, d), lambda b: (b, 0, 0)))
    scratch = [pltpu.VMEM((lc + 2 * HALO, w), F32)] + [pltpu.VMEM((lc, w), F32) for _ in range(6)]
    return pl.pallas_call(
        functools.partial(_ctx_kernel, with_ctx, alpha, bsz),
        grid=(bsz,),
        in_specs=in_specs,
        out_specs=out_specs,
        out_shape=out_shape,
        scratch_shapes=scratch,
        compiler_params=pltpu.CompilerParams(dimension_semantics=("arbitrary",), vmem_limit_bytes=VMEM_LIMIT),
        name="context_layer" if with_ctx else "context_kv_state",
    )(*ins)


def _scan_pitch(tl):
    return tl + 4


def _stage_coeffs(a, bco, a_scr, b_scr, nb, tl):
    pitch = _scan_pitch(tl)
    for s in range(W_BRANCH // LANES):
        for b in range(nb):
            a_scr[s, pl.ds(b * pitch, tl), :] = a[b * tl:(b + 1) * tl, s * LANES:(s + 1) * LANES]
            b_scr[s, pl.ds(b * pitch, tl), :] = bco[b * tl:(b + 1) * tl, s * LANES:(s + 1) * LANES]


def _run_scan(a_scr, b_scr, h_scr, hst_scr, nb, tl, reverse):
    pitch = _scan_pitch(tl)
    n_slab = W_BRANCH // LANES

    def body(t, hs):
        tt = (tl - 1 - t) if reverse else t
        new = []
        for s in range(n_slab):
            rows = pl.ds(tt, nb, stride=pitch)
            h = a_scr[s, rows, :] * hs[s] + b_scr[s, rows, :]
            h_scr[s, rows, :] = h
            new.append(h)
        return tuple(new)

    hs = lax.fori_loop(0, tl, body, tuple(hst_scr[s] for s in range(n_slab)), unroll=4)
    for s in range(n_slab):
        hst_scr[s] = hs[s]


def _scan_fwd_kernel(x_ref, xp_ref, xn_ref, mod_ref, w1_ref, b1_ref, cw_ref, cb_ref, gw_ref, gba_ref, gbx_ref,
                     lam_ref, h0_ref, cos_ref, slo_ref, shi_ref,
                     xl_ref, hf_ref, kv_ref,
                     a_scr, b_scr, h_scr, ext_scr, hst_scr):
    i = pl.program_id(0)
    nt = pl.num_programs(0)
    nb, tl, d = x_ref.shape
    w = W_BRANCH
    pitch = _scan_pitch(tl)

    def modulate(xv, b):
        return (xv * (1.0 + mod_ref[b:b + 1, d:2 * d]) + mod_ref[b:b + 1, 0:d]).astype(BF16)

    u = jnp.concatenate([modulate(x_ref[b], b) for b in range(nb)], axis=0)
    u_halo = jnp.concatenate([modulate(xp_ref[b], b) for b in range(nb)]
                             + [modulate(xn_ref[b], b) for b in range(nb)], axis=0)

    z = _dotf(u, w1_ref[...]) + b1_ref[...]
    bx = z[:, 0:w]
    bx_halo = _dotf(u_halo, w1_ref[:, 0:w]) + b1_ref[:, 0:w]
    bx_prev = jnp.where(i == 0, 0.0, bx_halo[0:nb * HALO])
    bx_next = jnp.where(i == nt - 1, 0.0, bx_halo[nb * HALO:2 * nb * HALO])

    cos, slo, shi = cos_ref[...], slo_ref[...], shi_ref[...]
    for b in range(nb):
        kv_ref[0, b] = _rope(z[b * tl:(b + 1) * tl, w:2 * w], cos, slo, shi).astype(BF16)
        kv_ref[1, b] = z[b * tl:(b + 1) * tl, 2 * w:3 * w].astype(BF16)

    cw, cb = cw_ref[...], cb_ref[...]
    xls = []
    for b in range(nb):
        ext_scr[b, pl.ds(0, HALO), :] = bx_prev[b * HALO:(b + 1) * HALO]
        ext_scr[b, pl.ds(HALO, tl), :] = bx[b * tl:(b + 1) * tl]
        ext_scr[b, pl.ds(HALO + tl, HALO), :] = bx_next[b * HALO:(b + 1) * HALO]
        xl_b = _conv_from_ext(ext_scr.at[b], tl, cw, cb)
        xl_ref[b] = xl_b
        xls.append(xl_b)
    xl = jnp.concatenate(xls, axis=0)

    a, bco = _lru_coeffs(xl, gw_ref[0, 0], gw_ref[0, 1], gba_ref[0], gbx_ref[0], lam_ref[0])
    _stage_coeffs(a, bco, a_scr, b_scr, nb, tl)

    @pl.when(i == 0)
    def _():
        for s in range(w // LANES):
            hst_scr[s] = h0_ref[0, :, s * LANES:(s + 1) * LANES]

    _run_scan(a_scr, b_scr, h_scr, hst_scr, nb, tl, reverse=False)
    for s in range(w // LANES):
        for b in range(nb):
            hf_ref[b, :, s * LANES:(s + 1) * LANES] = h_scr[s, pl.ds(b * pitch, tl), :]


def _scan_rev_kernel(xl_ref, hf_ref, gw_ref, gba_ref, gbx_ref, lam_ref, h0_ref,
                     y_ref,
                     a_scr, b_scr, h_scr, hst_scr):
    i = pl.program_id(0)
    nb, tl, w = xl_ref.shape
    pitch = _scan_pitch(tl)
    xl = jnp.concatenate([xl_ref[b] for b in range(nb)], axis=0)
    a, bco = _lru_coeffs(xl, gw_ref[0, 0], gw_ref[0, 1], gba_ref[0], gbx_ref[0], lam_ref[0])
    _stage_coeffs(a, bco, a_scr, b_scr, nb, tl)

    @pl.when(i == 0)
    def _():
        for s in range(w // LANES):
            hst_scr[s] = h0_ref[0, :, s * LANES:(s + 1) * LANES]

    _run_scan(a_scr, b_scr, h_scr, hst_scr, nb, tl, reverse=True)
    for s in range(w // LANES):
        ls = slice(s * LANES, (s + 1) * LANES)
        for b in range(nb):
            y_ref[b, :, ls] = hf_ref[b, :, ls] + h_scr[s, pl.ds(b * pitch, tl), :]


def _dir_spec(a, direction):
    nd = a.ndim
    return pl.BlockSpec((1,) + a.shape[1:], lambda i: (direction,) + (0,) * (nd - 1), pipeline_mode=pl.Buffered(1))


def _scan_scratch(bsz, tl):
    n_slab = W_BRANCH // LANES
    rows = bsz * _scan_pitch(tl)
    return [pltpu.VMEM((n_slab, rows, LANES), F32) for _ in range(3)]


def _scan_calls(x, mod, h0, p, tabs):
    bsz, seq, d = x.shape
    w = W_BRANCH
    tl = SCAN_TILE
    nt = seq // tl
    hb = tl // HALO
    n_halo = seq // HALO
    n_slab = w // LANES
    params = pltpu.CompilerParams(dimension_semantics=("arbitrary",), vmem_limit_bytes=VMEM_LIMIT)
    gate_specs = lambda direction: [_dir_spec(p["gw"], direction), _dir_spec(p["gba"], direction),
                                    _dir_spec(p["gbx"], direction), _dir_spec(p["lam"], direction),
                                    _dir_spec(h0, direction)]
    gate_args = [p["gw"], p["gba"], p["gbx"], p["lam"], h0]
    tile_spec = lambda last: pl.BlockSpec((bsz, tl, last), lambda i: (0, i, 0))
    tab_spec = pl.BlockSpec((tl, LANES), lambda i: (i, 0))

    xl, hf, kv = pl.pallas_call(
        _scan_fwd_kernel,
        grid=(nt,),
        in_specs=[tile_spec(d),
                  pl.BlockSpec((bsz, HALO, d), lambda i: (0, jnp.maximum(i * hb - 1, 0), 0)),
                  pl.BlockSpec((bsz, HALO, d), lambda i: (0, jnp.minimum((i + 1) * hb, n_halo - 1), 0)),
                  _const_spec(mod.shape), _const_spec(p["w1"].shape), _const_spec(p["b1"].shape),
                  _const_spec(p["conv_w"].shape), _const_spec(p["conv_b"].shape)]
                 + gate_specs(0) + [tab_spec, tab_spec, tab_spec],
        out_specs=[tile_spec(w), tile_spec(w), pl.BlockSpec((2, bsz, tl, w), lambda i: (0, 0, i, 0))],
        out_shape=[jax.ShapeDtypeStruct((bsz, seq, w), F32), jax.ShapeDtypeStruct((bsz, seq, w), F32),
                   jax.ShapeDtypeStruct((2, bsz, seq, w), BF16)],
        scratch_shapes=_scan_scratch(bsz, tl) + [pltpu.VMEM((bsz, tl + 2 * HALO, w), F32),
                                                 pltpu.VMEM((n_slab, bsz, LANES), F32)],
        compiler_params=params,
        name="latent_scan_fwd",
    )(x, x, x, mod, p["w1"], p["b1"], p["conv_w"], p["conv_b"], *gate_args,
      tabs["cos"], tabs["sin_lo"], tabs["sin_hi"])

    rev_spec = pl.BlockSpec((bsz, tl, w), lambda i: (0, nt - 1 - i, 0))
    yb = pl.pallas_call(
        _scan_rev_kernel,
        grid=(nt,),
        in_specs=[rev_spec, rev_spec] + gate_specs(1),
        out_specs=rev_spec,
        out_shape=jax.ShapeDtypeStruct((bsz, seq, w), F32),
        scratch_shapes=_scan_scratch(bsz, tl) + [pltpu.VMEM((n_slab, bsz, LANES), F32)],
        compiler_params=params,
        name="latent_scan_rev",
    )(xl, hf, *gate_args)
    return yb, kv


def _main_kernel(alpha, x_ref, yb_ref, kp_ref, kc_ref, kn_ref, vp_ref, vc_ref, vn_ref, kx_ref, vx_ref, mod_ref,
                 w2_ref, b2_ref, lng_ref, lnb_ref, ws_ref, bs_ref, wbr_ref, wout_ref, og_ref, ob_ref,
                 cos_ref, slo_ref, shi_ref, bm_ref, rm_ref,
                 o_ref,
                 kbuf, vbuf, q_scr, yc_scr, kx_scr, vx_scr):
    b = pl.program_id(0)
    i = pl.program_id(1)
    nt = pl.num_programs(1)
    tl, d = x_ref.shape[1], x_ref.shape[2]
    w = W_BRANCH
    half = kp_ref.shape[2]
    n_rows = tl // GRID_W

    x = x_ref[0]
    mod = mod_ref[pl.ds(b, 1), :]
    shift, scale, gate = mod[:, 0:d], mod[:, d:2 * d], mod[:, 2 * d:3 * d]
    u = (x * (1.0 + scale) + shift).astype(BF16)

    def proj(c0, n):
        return _dotf(u, w2_ref[:, c0:c0 + n]) + b2_ref[:, c0:c0 + n]

    y_a = _branch_a(proj(_OFF_AU, w), proj(_OFF_AV, w), proj(_OFF_AG, w),
                    lng_ref[...], lnb_ref[...], ws_ref, bs_ref[...])
    y_b = yb_ref[0] * _silu(proj(_OFF_BG, w))

    q = (_rope(proj(_OFF_CQ, w), cos_ref[...], slo_ref[...], shi_ref[...]) * (C_HD ** -0.5)).astype(BF16)
    n_pairs = w // LANES
    for p in range(n_pairs):
        ls = slice(p * LANES, (p + 1) * LANES)
        q_scr[p] = q[:, ls]
        kbuf[p, pl.ds(0, half), :] = kp_ref[0, 0, :, ls]
        kbuf[p, pl.ds(half, tl), :] = kc_ref[0, 0, :, ls]
        kbuf[p, pl.ds(half + tl, half), :] = kn_ref[0, 0, :, ls]
        vbuf[p, pl.ds(0, half), :] = vp_ref[0, 0, :, ls]
        vbuf[p, pl.ds(half, tl), :] = vc_ref[0, 0, :, ls]
        vbuf[p, pl.ds(half + tl, half), :] = vn_ref[0, 0, :, ls]
        kx_scr[p] = kx_ref[0, :, ls]
        vx_scr[p] = vx_ref[0, :, ls]

    hrows = n_rows // 2
    wrows = hrows + WIN_R
    total_rows = nt * n_rows
    codes = []
    for hf in range(2):
        per_rq = []
        for rq in range(hrows):
            r = i * n_rows + hf * hrows + rq
            w_lo = jnp.clip(r - WIN_R // 2, 0, total_rows - WIN_R) - (r - WIN_R // 2) + rq
            per_j = []
            for j in range(wrows // 2):
                va = ((2 * j >= w_lo) & (2 * j < w_lo + WIN_R)).astype(jnp.int32)
                vb = ((2 * j + 1 >= w_lo) & (2 * j + 1 < w_lo + WIN_R)).astype(jnp.int32)
                per_j.append(2 * va + vb)
            per_rq.append(per_j)
        codes.append(per_rq)

    nq = hrows * GRID_W
    nk = wrows * GRID_W

    def pair_body(p, carry):
        kx = kx_scr[p]
        vx = vx_scr[p]
        for hf in range(2):
            qp = q_scr[p, pl.ds(hf * nq, nq), :]
            kw = kbuf[p, pl.ds(hf * nq, nk), :]
            vw = vbuf[p, pl.ds(hf * nq, nk), :]
            acc = None
            for hh in range(2):
                msk = _head_mask(nq, hh == 1)
                qm = jnp.where(msk, qp, jnp.zeros_like(qp))
                s_loc = _dot_nt(qm, kw)
                s_ctx = _dot_nt(qm, kx)
                blocks = []
                for rq in range(hrows):
                    row = []
                    for j in range(wrows // 2):
                        blk = s_loc[rq * GRID_W:(rq + 1) * GRID_W, j * LANES:(j + 1) * LANES]
                        bias = bm_ref[2 * p + hh, 2 * j - rq + hrows - 1] + rm_ref[codes[hf][rq][j]]
                        row.append(blk + bias)
                    blocks.append(jnp.concatenate(row, axis=1))
                s_loc = jnp.concatenate(blocks, axis=0)
                mx = jnp.maximum(jnp.max(s_loc, axis=-1, keepdims=True), jnp.max(s_ctx, axis=-1, keepdims=True))
                e_loc = jnp.exp(s_loc - mx)
                e_ctx = jnp.exp(s_ctx - mx)
                den = jnp.sum(e_loc, axis=-1, keepdims=True) + jnp.sum(e_ctx, axis=-1, keepdims=True)
                o = (_dotf(e_loc.astype(BF16), vw) + _dotf(e_ctx.astype(BF16), vx)) / den
                acc = o if acc is None else jnp.where(msk, o, acc)
            yc_scr[p, pl.ds(hf * nq, nq), :] = acc
        return carry

    lax.fori_loop(0, n_pairs, pair_body, 0)
    y_c = jnp.concatenate([yc_scr[p] for p in range(n_pairs)], axis=1) * _silu(proj(_OFF_CG, w))

    o_ref[0] = _merge_tail(x, u, gate, (y_a, y_b, y_c), w2_ref, b2_ref, wbr_ref, wout_ref,
                           og_ref[...], ob_ref[...], alpha)


def _main_call(alpha, x, yb, kv, kx, vx, mod, p, tabs):
    bsz, seq, d = x.shape
    w = W_BRANCH
    tl = MAIN_ROWS * GRID_W
    half = tl // 2
    nt = seq // tl
    n_half = seq // half
    lc = kx.shape[1]

    consts = [mod, p["w2"], p["b2"], p["sgu_g"], p["sgu_b"], p["ws"], p["bs_tab"], p["wbr"], p["wout"],
              p["ln_g"], p["ln_b"]]

    def window_specs(j):
        return [pl.BlockSpec((1, 1, half, w), lambda b, i: (j, b, jnp.maximum(2 * i - 1, 0), 0)),
                pl.BlockSpec((1, 1, tl, w), lambda b, i: (j, b, i, 0)),
                pl.BlockSpec((1, 1, half, w), lambda b, i: (j, b, jnp.minimum(2 * i + 2, n_half - 1), 0))]

    ctx_spec = pl.BlockSpec((1, lc, w), lambda b, i: (b, 0, 0))
    tab_spec = pl.BlockSpec((tl, LANES), lambda b, i: (i, 0))
    in_specs = ([pl.BlockSpec((1, tl, d), lambda b, i: (b, i, 0)),
                 pl.BlockSpec((1, tl, w), lambda b, i: (b, i, 0))]
                + window_specs(0) + window_specs(1) + [ctx_spec, ctx_spec]
                + [_const_spec(a.shape) for a in consts]
                + [tab_spec, tab_spec, tab_spec, _const_spec(tabs["bias"].shape),
                   _const_spec(tabs["row_mask"].shape)])
    n_pairs = w // LANES
    scratch = [pltpu.VMEM((n_pairs, 2 * tl, LANES), BF16), pltpu.VMEM((n_pairs, 2 * tl, LANES), BF16),
               pltpu.VMEM((n_pairs, tl, LANES), BF16), pltpu.VMEM((n_pairs, tl, LANES), F32),
               pltpu.VMEM((n_pairs, lc, LANES), BF16), pltpu.VMEM((n_pairs, lc, LANES), BF16)]
    return pl.pallas_call(
        functools.partial(_main_kernel, alpha),
        grid=(bsz, nt),
        in_specs=in_specs,
        out_specs=pl.BlockSpec((1, tl, d), lambda b, i: (b, i, 0)),
        out_shape=jax.ShapeDtypeStruct((bsz, seq, d), F32),
        scratch_shapes=scratch,
        compiler_params=pltpu.CompilerParams(dimension_semantics=("arbitrary", "arbitrary"),
                                             vmem_limit_bytes=VMEM_LIMIT),
        name="latent_main",
    )(x, yb, kv, kv, kv, kv, kv, kv, kx, vx, *consts, tabs["cos"], tabs["sin_lo"], tabs["sin_hi"], tabs["bias"],
      tabs["row_mask"])


def _block_diag_groups(wg):
    per = GATE_GROUP // B_BW
    g = B_BLOCKS // per
    wg = wg.reshape(2, g, per, B_BW, B_BW)
    eye = jnp.eye(per, dtype=wg.dtype)
    return jnp.einsum("dghij,hk->dghikj", wg, eye).reshape(2, g, GATE_GROUP, GATE_GROUP)


def _rope_tables(seq):
    pos = jnp.arange(seq)
    rows, cols = (pos // GRID_W).astype(F32), (pos % GRID_W).astype(F32)
    quarter = C_HD // 4
    inv_freq = ROPE_BASE ** (-jnp.arange(quarter, dtype=F32) / quarter)
    lane = np.arange(LANES) % C_HD
    use_row = lane < C_HD // 2
    first = (lane % (C_HD // 2)) < quarter
    inv_lane = jnp.concatenate([inv_freq] * (LANES // quarter))
    ang = jnp.where(use_row[None, :], rows[:, None], cols[:, None]) * inv_lane[None, :]
    sin = jnp.sin(ang)
    return {"cos": jnp.cos(ang),
            "sin_lo": jnp.where(first[None, :], -sin, 0.0),
            "sin_hi": jnp.where(first[None, :], 0.0, sin)}


def _bias_table(rpb_l):
    qc = np.arange(GRID_W)
    c0 = np.clip(qc - WIN_C // 2, 0, GRID_W - WIN_C)
    kc = np.arange(GRID_W)
    allowed = (kc[None, :] >= c0[:, None]) & (kc[None, :] < c0[:, None] + WIN_C)
    cidx = np.clip(kc[None, :] - qc[:, None] + (WIN_C - 1), 0, 2 * WIN_C - 2)
    t = None
    for cc in range(2 * WIN_C - 1):
        sel = jnp.asarray((cidx == cc) & allowed, F32)
        term = rpb_l[:, :, cc][:, :, None, None] * sel[None, None]
        t = term if t is None else t + term
    t = jnp.where(allowed[None, None], t, NEG_BIG)
    return jnp.concatenate([t[:, :-1], t[:, 1:]], axis=-1)


def _row_mask_table():
    lane_first = np.arange(LANES) < GRID_W
    tab = np.zeros((4, GRID_W, LANES), np.float32)
    for a in range(2):
        for b in range(2):
            ok = np.where(lane_first, bool(a), bool(b))
            tab[2 * a + b] = np.where(ok, 0.0, NEG_BIG)[None, :]
    return jnp.asarray(tab)


def _layer_params(l, w_in, b_in, sgu_ln_g, sgu_ln_b, w_s, b_s, conv_w, conv_b, lru_wa, lru_ba, lru_wx, lru_bx,
                  lru_lam, w_br, w_out, ln_g, ln_b):
    w = W_BRANCH
    wi, bi = w_in[l], b_in[l]
    cols1 = lambda a: jnp.concatenate([a[..., 3 * w:4 * w], a[..., 6 * w:8 * w]], axis=-1)
    cols2 = lambda a: jnp.concatenate([a[..., 0:3 * w], a[..., 4 * w:6 * w], a[..., 8 * w:]], axis=-1)
    gw = jnp.concatenate([_block_diag_groups(lru_wa[l]), _block_diag_groups(lru_wx[l])], axis=-1)
    return {
        "w1": cols1(wi).astype(BF16), "b1": cols1(bi)[None, :],
        "w2": cols2(wi).astype(BF16), "b2": cols2(bi)[None, :],
        "conv_w": conv_w[l], "conv_b": conv_b[l][None, :],
        "gw": gw.astype(BF16), "gba": lru_ba[l][:, None, :], "gbx": lru_bx[l][:, None, :],
        "lam": lru_lam[l][:, None, :],
        "sgu_g": sgu_ln_g[l][None, :], "sgu_b": sgu_ln_b[l][None, :],
        "ws": w_s[l].astype(BF16), "bs_tab": jnp.repeat(b_s[l].T, LANES, axis=1),
        "wbr": w_br[l].astype(BF16), "wout": w_out[l].astype(BF16),
        "ln_g": ln_g[l][None, :], "ln_b": ln_b[l][None, :],
    }


def kernel(x, c, ctx, c_ctx, w_ada, b_ada, w_in, b_in, sgu_ln_g, sgu_ln_b, w_s, b_s, conv_w, conv_b, lru_wa,
           lru_ba, lru_wx, lru_bx, lru_lam, rpb, w_br, w_out, ln_g, ln_b):
    bsz, seq, d = x.shape
    depth = w_ada.shape[0]
    assert d % LANES == 0 and seq % (MAIN_ROWS * GRID_W) == 0 and seq // GRID_W >= 2 * MAIN_ROWS
    assert seq % SCAN_TILE == 0 and ctx.shape[1] % CHUNK == 0 and bsz <= 8
    alpha = (2 * depth) ** 0.25

    pad = jnp.zeros((16 - bsz - 1, d), F32)
    mods = _ada_call(jnp.concatenate([c, c_ctx[None, :], pad], axis=0), w_ada, b_ada)
    tabs = _rope_tables(seq)

    xc = ctx
    for l in range(depth):
        with_ctx = l < depth - 1
        p = _layer_params(l, w_in, b_in, sgu_ln_g, sgu_ln_b, w_s, b_s, conv_w, conv_b, lru_wa, lru_ba, lru_wx,
                          lru_bx, lru_lam, w_br, w_out, ln_g, ln_b)
        ctx_out = _ctx_call(with_ctx, alpha, xc, mods[l], p)
        kx, vx, hfin = ctx_out[0], ctx_out[1], ctx_out[2]
        h0 = hfin.transpose(1, 0, 2)
        yb, kv = _scan_calls(x, mods[l], h0, p, tabs)
        layer_tabs = dict(tabs, bias=_bias_table(rpb[l]), row_mask=_row_mask_table())
        x = _main_call(alpha, x, yb, kv, kx, vx, mods[l], p, layer_tabs)
        if with_ctx:
            xc = ctx_out[3]
    return x
```

```python
import functools

import numpy as np
import jax
import jax.numpy as jnp
from jax import lax
from jax.experimental import pallas as pl
from jax.experimental.pallas import tpu as pltpu

F32 = jnp.float32
BF16 = jnp.bfloat16

GRID_W = 64
W_BRANCH = 512
N_BRANCH = 3
CHUNK = 128
A_GROUPS = 4
B_BLOCKS = 8
B_BW = W_BRANCH // B_BLOCKS
CONV_W = 4
CONV_PAD_L = 2
LRU_C = 8.0
C_HEADS = 8
C_HD = W_BRANCH // C_HEADS
WIN_R = 8
WIN_C = 16
ROPE_BASE = 10000.0
LN_EPS = 1e-5
NEG_BIG = -1e30
LOG2E = 1.4426950408889634

LANES = 128
HALO = 8
GATE_GROUP = 256
SCAN_TILE = 128
MAIN_ROWS = 8
VMEM_LIMIT = 60 * 1024 * 1024

_OFF_AU, _OFF_AV, _OFF_AG, _OFF_BG, _OFF_CQ, _OFF_CG, _OFF_GM = 0, 512, 1024, 1536, 2048, 2560, 3072


def _dotf(a, b):
    return jnp.dot(a, b, preferred_element_type=F32)


def _dot_nt(a, b):
    return lax.dot_general(a, b, (((1,), (1,)), ((), ())), preferred_element_type=F32)


def _sigmoid(x):
    return 1.0 / (1.0 + jnp.exp(-x))


def _silu(x):
    return x * _sigmoid(x)


def _gelu(x):
    return 0.5 * x * (1.0 + lax.erf(x * 0.7071067811865476))


def _softplus(x):
    return jnp.maximum(x, 0.0) + jnp.log1p(jnp.exp(-jnp.abs(x)))


def _ln(x, g, b):
    mu = jnp.mean(x, axis=-1, keepdims=True)
    xc = x - mu
    var = jnp.mean(xc * xc, axis=-1, keepdims=True)
    return xc * lax.rsqrt(var + LN_EPS) * g + b


def _rope(x, cos, sin_lo, sin_hi):
    outs = []
    for s in range(x.shape[1] // LANES):
        xs = x[:, s * LANES:(s + 1) * LANES]
        outs.append(xs * cos + pltpu.roll(xs, LANES - 16, 1) * sin_lo + pltpu.roll(xs, 16, 1) * sin_hi)
    return jnp.concatenate(outs, axis=1)


def _conv_from_ext(ext_ref, n, cw, cb):
    acc = cb
    for j in range(CONV_W):
        acc = acc + cw[j:j + 1, :] * ext_ref[pl.ds(HALO - CONV_PAD_L + j, n), :]
    return acc


def _lru_coeffs(xl, gw0, gw1, ba, bx, lam):
    xb = xl.astype(BF16)
    o0 = _dotf(xb[:, :GATE_GROUP], gw0)
    o1 = _dotf(xb[:, GATE_GROUP:], gw1)
    r = _sigmoid(jnp.concatenate([o0[:, :GATE_GROUP], o1[:, :GATE_GROUP]], axis=1) + ba)
    i = _sigmoid(jnp.concatenate([o0[:, GATE_GROUP:], o1[:, GATE_GROUP:]], axis=1) + bx)
    log_a = (-LRU_C) * r * _softplus(-lam)
    a = jnp.exp(log_a)
    t = jnp.tanh(log_a)
    one_minus_a2 = (-2.0 * t) / (1.0 - t)
    return a, jnp.sqrt(one_minus_a2) * (i * xl)


def _branch_a(zu, zv, zg, lng, lnb, ws_ref, bs_tab):
    gu = _gelu(zu)
    gv = _ln(_gelu(zv), lng, lnb).astype(BF16)
    rows = []
    for n in range(zu.shape[0] // CHUNK):
        cols = [_dotf(ws_ref[g], gv[n * CHUNK:(n + 1) * CHUNK, g * LANES:(g + 1) * LANES])
                for g in range(A_GROUPS)]
        rows.append(jnp.concatenate(cols, axis=1) + bs_tab)
    return gu * jnp.concatenate(rows, axis=0) * _silu(zg)


def _head_mask(rows, second):
    lane = lax.broadcasted_iota(jnp.int32, (rows, LANES), 1)
    return (lane >= C_HD) if second else (lane < C_HD)


def _ada_kernel(cc_ref, w_ref, b_ref, o_ref):
    s = _silu(cc_ref[...])
    o_ref[0] = jnp.dot(s, w_ref[0], preferred_element_type=F32, precision=lax.Precision.HIGHEST) + b_ref[0]


def _ada_call(cc, w_ada, b_ada):
    depth, d, n3 = w_ada.shape
    nb = n3 // d
    return pl.pallas_call(
        _ada_kernel,
        grid=(depth, nb),
        in_specs=[pl.BlockSpec(cc.shape, lambda l, j: (0, 0)),
                  pl.BlockSpec((1, d, d), lambda l, j: (l, 0, j)),
                  pl.BlockSpec((1, 1, d), lambda l, j: (l, 0, j))],
        out_specs=pl.BlockSpec((1, cc.shape[0], d), lambda l, j: (l, 0, j)),
        out_shape=jax.ShapeDtypeStruct((depth, cc.shape[0], n3), F32),
        compiler_params=pltpu.CompilerParams(dimension_semantics=("arbitrary", "arbitrary"),
                                             vmem_limit_bytes=VMEM_LIMIT),
        name="adaln_modulation",
    )(cc, w_ada, b_ada.reshape(depth, 1, n3))


def _merge_tail(x, u, gate, ys, w2_ref, b2_ref, wbr_ref, wout_ref, og, ob, alpha):
    m = None
    for n in range(N_BRANCH):
        c0 = _OFF_GM + n * x.shape[1]
        g = _sigmoid(_dotf(u, w2_ref[:, c0:c0 + x.shape[1]]) + b2_ref[:, c0:c0 + x.shape[1]])
        t = g * _dotf(ys[n].astype(BF16), wbr_ref[n])
        m = t if m is None else m + t
    return _ln(alpha * x + gate * _dotf(m.astype(BF16), wout_ref[...]), og, ob)


def _ctx_kernel(with_ctx, alpha, n_batch, *refs):
    (xc_ref, mod_ref, w1_ref, b1_ref, cw_ref, cb_ref, gw_ref, gba_ref, gbx_ref, lam_ref) = refs[:10]
    if with_ctx:
        (w2_ref, b2_ref, lng_ref, lnb_ref, ws_ref, bs_ref, wbr_ref, wout_ref, og_ref, ob_ref) = refs[10:20]
        k_ref, v_ref, hfin_ref, xo_ref = refs[20:24]
        ext_scr, af_scr, bf_scr, ar_scr, br_scr, hf_scr, hr_scr = refs[24:]
    else:
        k_ref, v_ref, hfin_ref = refs[10:13]
        ext_scr, af_scr, bf_scr, ar_scr, br_scr, hf_scr, hr_scr = refs[13:]
    lc, d = xc_ref.shape[1], xc_ref.shape[2]
    w = W_BRANCH

    xc = xc_ref[0]
    mod = mod_ref[n_batch:n_batch + 1, :]
    shift, scale, gate = mod[:, 0:d], mod[:, d:2 * d], mod[:, 2 * d:3 * d]
    u = (xc * (1.0 + scale) + shift).astype(BF16)

    z1 = _dotf(u, w1_ref[...]) + b1_ref[...]
    bx = z1[:, 0:w]
    kc = z1[:, w:2 * w].astype(BF16)
    vc = z1[:, 2 * w:3 * w].astype(BF16)
    k_ref[0] = kc
    v_ref[0] = vc

    zeros = jnp.zeros((HALO, w), F32)
    ext_scr[pl.ds(0, HALO), :] = zeros
    ext_scr[pl.ds(HALO, lc), :] = bx
    ext_scr[pl.ds(HALO + lc, HALO), :] = zeros
    xl = _conv_from_ext(ext_scr, lc, cw_ref[...], cb_ref[...])

    a_f, b_f = _lru_coeffs(xl, gw_ref[0, 0], gw_ref[0, 1], gba_ref[0], gbx_ref[0], lam_ref[0])
    a_r, b_r = _lru_coeffs(xl, gw_ref[1, 0], gw_ref[1, 1], gba_ref[1], gbx_ref[1], lam_ref[1])
    af_scr[...] = a_f
    bf_scr[...] = b_f
    ar_scr[...] = a_r
    br_scr[...] = b_r

    def body(t, carry):
        hf, hr = carry
        tr = lc - 1 - t
        hf = af_scr[pl.ds(t, 1), :] * hf + bf_scr[pl.ds(t, 1), :]
        hr = ar_scr[pl.ds(tr, 1), :] * hr + br_scr[pl.ds(tr, 1), :]
        hf_scr[pl.ds(t, 1), :] = hf
        hr_scr[pl.ds(tr, 1), :] = hr
        return hf, hr

    h0 = jnp.zeros((1, w), F32)
    hf, hr = lax.fori_loop(0, lc, body, (h0, h0))
    hfin_ref[0, 0:1, :] = hf
    hfin_ref[0, 1:2, :] = hr

    if not with_ctx:
        return

    def proj(c0, n):
        return _dotf(u, w2_ref[:, c0:c0 + n]) + b2_ref[:, c0:c0 + n]

    y_a = _branch_a(proj(_OFF_AU, w), proj(_OFF_AV, w), proj(_OFF_AG, w),
                    lng_ref[...], lnb_ref[...], ws_ref, bs_ref[...])
    y_b = (hf_scr[...] + hr_scr[...]) * _silu(proj(_OFF_BG, w))

    q = (proj(_OFF_CQ, w) * (C_HD ** -0.5)).astype(BF16)
    outs = []
    for p in range(w // LANES):
        qp = q[:, p * LANES:(p + 1) * LANES]
        kp = kc[:, p * LANES:(p + 1) * LANES]
        vp = vc[:, p * LANES:(p + 1) * LANES]
        acc = None
        for hh in range(2):
            msk = _head_mask(lc, hh == 1)
            s = _dot_nt(jnp.where(msk, qp, jnp.zeros_like(qp)), kp)
            e = jnp.exp(s - jnp.max(s, axis=-1, keepdims=True))
            o = _dotf(e.astype(BF16), vp) / jnp.sum(e, axis=-1, keepdims=True)
            acc = o if acc is None else jnp.where(msk, o, acc)
        outs.append(acc)
    y_c = jnp.concatenate(outs, axis=1) * _silu(proj(_OFF_CG, w))

    xo_ref[0] = _merge_tail(xc, u, gate, (y_a, y_b, y_c), w2_ref, b2_ref, wbr_ref, wout_ref,
                            og_ref[...], ob_ref[...], alpha)


def _const_spec(shape):
    nd = len(shape)
    return pl.BlockSpec(shape, lambda *_: (0,) * nd, pipeline_mode=pl.Buffered(1))


def _ctx_call(with_ctx, alpha, xc, mod, p):
    bsz, lc, d = xc.shape
    w = W_BRANCH
    ins = [xc, mod, p["w1"], p["b1"], p["conv_w"], p["conv_b"], p["gw"], p["gba"], p["gbx"], p["lam"]]
    in_specs = [pl.BlockSpec((1, lc, d), lambda b: (b, 0, 0))] + [_const_spec(a.shape) for a in ins[1:]]
    out_shape = [jax.ShapeDtypeStruct((bsz, lc, w), BF16), jax.ShapeDtypeStruct((bsz, lc, w), BF16),
                 jax.ShapeDtypeStruct((bsz, 2, w), F32)]
    out_specs = [pl.BlockSpec((1, lc, w), lambda b: (b, 0, 0)), pl.BlockSpec((1, lc, w), lambda b: (b, 0, 0)),
                 pl.BlockSpec((1, 2, w), lambda b: (b, 0, 0))]
    if with_ctx:
        extra = [p["w2"], p["b2"], p["sgu_g"], p["sgu_b"], p["ws"], p["bs_tab"], p["wbr"], p["wout"],
                 p["ln_g"], p["ln_b"]]
        ins += extra
        in_specs += [_const_spec(a.shape) for a in extra]
        out_shape.append(jax.ShapeDtypeStruct((bsz, lc, d), F32))
        out_specs.append(pl.BlockSpec((1, lc, d), lambda b: (b, 0, 0)))
    scratch = [pltpu.VMEM((lc + 2 * HALO, w), F32)] + [pltpu.VMEM((lc, w), F32) for _ in range(6)]
    return pl.pallas_call(
        functools.partial(_ctx_kernel, with_ctx, alpha, bsz),
        grid=(bsz,),
        in_specs=in_specs,
        out_specs=out_specs,
        out_shape=out_shape,
        scratch_shapes=scratch,
        compiler_params=pltpu.CompilerParams(dimension_semantics=("arbitrary",), vmem_limit_bytes=VMEM_LIMIT),
        name="context_layer" if with_ctx else "context_kv_state",
    )(*ins)


def _scan_pitch(tl):
    return tl + 4


def _stage_coeffs(a, bco, a_scr, b_scr, nb, tl):
    pitch = _scan_pitch(tl)
    for s in range(W_BRANCH // LANES):
        for b in range(nb):
            a_scr[s, pl.ds(b * pitch, tl), :] = a[b * tl:(b + 1) * tl, s * LANES:(s + 1) * LANES]
            b_scr[s, pl.ds(b * pitch, tl), :] = bco[b * tl:(b + 1) * tl, s * LANES:(s + 1) * LANES]


def _run_scan(a_scr, b_scr, h_scr, hst_scr, nb, tl, reverse):
    pitch = _scan_pitch(tl)
    n_slab = W_BRANCH // LANES

    def body(t, hs):
        tt = (tl - 1 - t) if reverse else t
        new = []
        for s in range(n_slab):
            rows = pl.ds(tt, nb, stride=pitch)
            h = a_scr[s, rows, :] * hs[s] + b_scr[s, rows, :]
            h_scr[s, rows, :] = h
            new.append(h)
        return tuple(new)

    hs = lax.fori_loop(0, tl, body, tuple(hst_scr[s] for s in range(n_slab)), unroll=4)
    for s in range(n_slab):
        hst_scr[s] = hs[s]


def _scan_fwd_kernel(x_ref, xp_ref, xn_ref, mod_ref, w1_ref, b1_ref, cw_ref, cb_ref, gw_ref, gba_ref, gbx_ref,
                     lam_ref, h0_ref, cos_ref, slo_ref, shi_ref,
                     xl_ref, hf_ref, kv_ref,
                     a_scr, b_scr, h_scr, ext_scr, hst_scr):
    i = pl.program_id(0)
    nt = pl.num_programs(0)
    nb, tl, d = x_ref.shape
    w = W_BRANCH
    pitch = _scan_pitch(tl)

    def modulate(xv, b):
        return (xv * (1.0 + mod_ref[b:b + 1, d:2 * d]) + mod_ref[b:b + 1, 0:d]).astype(BF16)

    u = jnp.concatenate([modulate(x_ref[b], b) for b in range(nb)], axis=0)
    u_halo = jnp.concatenate([modulate(xp_ref[b], b) for b in range(nb)]
                             + [modulate(xn_ref[b], b) for b in range(nb)], axis=0)

    z = _dotf(u, w1_ref[...]) + b1_ref[...]
    bx = z[:, 0:w]
    bx_halo = _dotf(u_halo, w1_ref[:, 0:w]) + b1_ref[:, 0:w]
    bx_prev = jnp.where(i == 0, 0.0, bx_halo[0:nb * HALO])
    bx_next = jnp.where(i == nt - 1, 0.0, bx_halo[nb * HALO:2 * nb * HALO])

    cos, slo, shi = cos_ref[...], slo_ref[...], shi_ref[...]
    for b in range(nb):
        kv_ref[0, b] = _rope(z[b * tl:(b + 1) * tl, w:2 * w], cos, slo, shi).astype(BF16)
        kv_ref[1, b] = z[b * tl:(b + 1) * tl, 2 * w:3 * w].astype(BF16)

    cw, cb = cw_ref[...], cb_ref[...]
    xls = []
    for b in range(nb):
        ext_scr[b, pl.ds(0, HALO), :] = bx_prev[b * HALO:(b + 1) * HALO]
        ext_scr[b, pl.ds(HALO, tl), :] = bx[b * tl:(b + 1) * tl]
        ext_scr[b, pl.ds(HALO + tl, HALO), :] = bx_next[b * HALO:(b + 1) * HALO]
        xl_b = _conv_from_ext(ext_scr.at[b], tl, cw, cb)
        xl_ref[b] = xl_b
        xls.append(xl_b)
    xl = jnp.concatenate(xls, axis=0)

    a, bco = _lru_coeffs(xl, gw_ref[0, 0], gw_ref[0, 1], gba_ref[0], gbx_ref[0], lam_ref[0])
    _stage_coeffs(a, bco, a_scr, b_scr, nb, tl)

    @pl.when(i == 0)
    def _():
        for s in range(w // LANES):
            hst_scr[s] = h0_ref[0, :, s * LANES:(s + 1) * LANES]

    _run_scan(a_scr, b_scr, h_scr, hst_scr, nb, tl, reverse=False)
    for s in range(w // LANES):
        for b in range(nb):
            hf_ref[b, :, s * LANES:(s + 1) * LANES] = h_scr[s, pl.ds(b * pitch, tl), :]


def _scan_rev_kernel(xl_ref, hf_ref, gw_ref, gba_ref, gbx_ref, lam_ref, h0_ref,
                     y_ref,
                     a_scr, b_scr, h_scr, hst_scr):
    i = pl.program_id(0)
    nb, tl, w = xl_ref.shape
    pitch = _scan_pitch(tl)
    xl = jnp.concatenate([xl_ref[b] for b in range(nb)], axis=0)
    a, bco = _lru_coeffs(xl, gw_ref[0, 0], gw_ref[0, 1], gba_ref[0], gbx_ref[0], lam_ref[0])
    _stage_coeffs(a, bco, a_scr, b_scr, nb, tl)

    @pl.when(i == 0)
    def _():
        for s in range(w // LANES):
            hst_scr[s] = h0_ref[0, :, s * LANES:(s + 1) * LANES]

    _run_scan(a_scr, b_scr, h_scr, hst_scr, nb, tl, reverse=True)
    for s in range(w // LANES):
        ls = slice(s * LANES, (s + 1) * LANES)
        for b in range(nb):
            y_ref[b, :, ls] = hf_ref[b, :, ls] + h_scr[s, pl.ds(b * pitch, tl), :]


def _dir_spec(a, direction):
    nd = a.ndim
    return pl.BlockSpec((1,) + a.shape[1:], lambda i: (direction,) + (0,) * (nd - 1), pipeline_mode=pl.Buffered(1))


def _scan_scratch(bsz, tl):
    n_slab = W_BRANCH // LANES
    rows = bsz * _scan_pitch(tl)
    return [pltpu.VMEM((n_slab, rows, LANES), F32) for _ in range(3)]


def _scan_calls(x, mod, h0, p, tabs):
    bsz, seq, d = x.shape
    w = W_BRANCH
    tl = SCAN_TILE
    nt = seq // tl
    hb = tl // HALO
    n_halo = seq // HALO
    n_slab = w // LANES
    params = pltpu.CompilerParams(dimension_semantics=("arbitrary",), vmem_limit_bytes=VMEM_LIMIT)
    gate_specs = lambda direction: [_dir_spec(p["gw"], direction), _dir_spec(p["gba"], direction),
                                    _dir_spec(p["gbx"], direction), _dir_spec(p["lam"], direction),
                                    _dir_spec(h0, direction)]
    gate_args = [p["gw"], p["gba"], p["gbx"], p["lam"], h0]
    tile_spec = lambda last: pl.BlockSpec((bsz, tl, last), lambda i: (0, i, 0))
    tab_spec = pl.BlockSpec((tl, LANES), lambda i: (i, 0))

    xl, hf, kv = pl.pallas_call(
        _scan_fwd_kernel,
        grid=(nt,),
        in_specs=[tile_spec(d),
                  pl.BlockSpec((bsz, HALO, d), lambda i: (0, jnp.maximum(i * hb - 1, 0), 0)),
                  pl.BlockSpec((bsz, HALO, d), lambda i: (0, jnp.minimum((i + 1) * hb, n_halo - 1), 0)),
                  _const_spec(mod.shape), _const_spec(p["w1"].shape), _const_spec(p["b1"].shape),
                  _const_spec(p["conv_w"].shape), _const_spec(p["conv_b"].shape)]
                 + gate_specs(0) + [tab_spec, tab_spec, tab_spec],
        out_specs=[tile_spec(w), tile_spec(w), pl.BlockSpec((2, bsz, tl, w), lambda i: (0, 0, i, 0))],
        out_shape=[jax.ShapeDtypeStruct((bsz, seq, w), F32), jax.ShapeDtypeStruct((bsz, seq, w), F32),
                   jax.ShapeDtypeStruct((2, bsz, seq, w), BF16)],
        scratch_shapes=_scan_scratch(bsz, tl) + [pltpu.VMEM((bsz, tl + 2 * HALO, w), F32),
                                                 pltpu.VMEM((n_slab, bsz, LANES), F32)],
        compiler_params=params,
        name="latent_scan_fwd",
    )(x, x, x, mod, p["w1"], p["b1"], p["conv_w"], p["conv_b"], *gate_args,
      tabs["cos"], tabs["sin_lo"], tabs["sin_hi"])

    rev_spec = pl.BlockSpec((bsz, tl, w), lambda i: (0, nt - 1 - i, 0))
    yb = pl.pallas_call(
        _scan_rev_kernel,
        grid=(nt,),
        in_specs=[rev_spec, rev_spec] + gate_specs(1),
        out_specs=rev_spec,
        out_shape=jax.ShapeDtypeStruct((bsz, seq, w), F32),
        scratch_shapes=_scan_scratch(bsz, tl) + [pltpu.VMEM((n_slab, bsz, LANES), F32)],
        compiler_params=params,
        name="latent_scan_rev",
    )(xl, hf, *gate_args)
    return yb, kv


def _main_kernel(alpha, x_ref, yb_ref, kp_ref, kc_ref, kn_ref, vp_ref, vc_ref, vn_ref, kx_ref, vx_ref, mod_ref,
                 w2_ref, b2_ref, lng_ref, lnb_ref, ws_ref, bs_ref, wbr_ref, wout_ref, og_ref, ob_ref,
                 cos_ref, slo_ref, shi_ref, bm_ref, rm_ref,
                 o_ref,
                 kbuf, vbuf, q_scr, yc_scr):
    b = pl.program_id(0)
    i = pl.program_id(1)
    nt = pl.num_programs(1)
    tl, d = x_ref.shape[1], x_ref.shape[2]
    w = W_BRANCH
    half = kp_ref.shape[2]
    n_rows = tl // GRID_W

    x = x_ref[0]
    mod = mod_ref[pl.ds(b, 1), :]
    shift, scale, gate = mod[:, 0:d], mod[:, d:2 * d], mod[:, 2 * d:3 * d]
    u = (x * (1.0 + scale) + shift).astype(BF16)

    def proj(c0, n):
        return _dotf(u, w2_ref[:, c0:c0 + n]) + b2_ref[:, c0:c0 + n]

    y_a = _branch_a(proj(_OFF_AU, w), proj(_OFF_AV, w), proj(_OFF_AG, w),
                    lng_ref[...], lnb_ref[...], ws_ref, bs_ref[...])
    y_b = yb_ref[0] * _silu(proj(_OFF_BG, w))

    q = (_rope(proj(_OFF_CQ, w), cos_ref[...], slo_ref[...], shi_ref[...]) * (C_HD ** -0.5 * LOG2E)).astype(BF16)
    n_pairs = w // LANES
    for p in range(n_pairs):
        ls = slice(p * LANES, (p + 1) * LANES)
        q_scr[p] = q[:, ls]
        kbuf[p, pl.ds(0, half), :] = kp_ref[0, 0, :, ls]
        kbuf[p, pl.ds(half, tl), :] = kc_ref[0, 0, :, ls]
        kbuf[p, pl.ds(half + tl, half), :] = kn_ref[0, 0, :, ls]
        vbuf[p, pl.ds(0, half), :] = vp_ref[0, 0, :, ls]
        vbuf[p, pl.ds(half, tl), :] = vc_ref[0, 0, :, ls]
        vbuf[p, pl.ds(half + tl, half), :] = vn_ref[0, 0, :, ls]

    hrows = n_rows // 2
    wrows = hrows + WIN_R
    total_rows = nt * n_rows
    codes = []
    for hf in range(2):
        per_rq = []
        for rq in range(hrows):
            r = i * n_rows + hf * hrows + rq
            w_lo = jnp.clip(r - WIN_R // 2, 0, total_rows - WIN_R) - (r - WIN_R // 2) + rq
            per_j = []
            for j in range(wrows // 2):
                va = ((2 * j >= w_lo) & (2 * j < w_lo + WIN_R)).astype(jnp.int32)
                vb = ((2 * j + 1 >= w_lo) & (2 * j + 1 < w_lo + WIN_R)).astype(jnp.int32)
                per_j.append(2 * va + vb)
            per_rq.append(per_j)
        codes.append(per_rq)

    nq = hrows * GRID_W
    nk = wrows * GRID_W

    def pair_body(p, carry):
        kx = kx_ref[0, :, p * LANES:(p + 1) * LANES]
        vx = vx_ref[0, :, p * LANES:(p + 1) * LANES]
        for hf in range(2):
            qp = q_scr[p, pl.ds(hf * nq, nq), :]
            kw = kbuf[p, pl.ds(hf * nq, nk), :]
            vw = vbuf[p, pl.ds(hf * nq, nk), :]
            acc = None
            for hh in range(2):
                msk = _head_mask(nq, hh == 1)
                qm = jnp.where(msk, qp, jnp.zeros_like(qp))
                s_loc = _dot_nt(qm, kw)
                s_ctx = _dot_nt(qm, kx)
                blocks = []
                for rq in range(hrows):
                    row = []
                    for j in range(wrows // 2):
                        blk = s_loc[rq * GRID_W:(rq + 1) * GRID_W, j * LANES:(j + 1) * LANES]
                        bias = bm_ref[2 * p + hh, 2 * j - rq + hrows - 1] + rm_ref[codes[hf][rq][j]]
                        row.append(blk + bias)
                    blocks.append(jnp.concatenate(row, axis=1))
                s_loc = jnp.concatenate(blocks, axis=0)
                mx = jnp.maximum(jnp.max(s_loc, axis=-1, keepdims=True), jnp.max(s_ctx, axis=-1, keepdims=True))
                e_loc = jnp.exp2(s_loc - mx)
                e_ctx = jnp.exp2(s_ctx - mx)
                den = jnp.sum(e_loc, axis=-1, keepdims=True) + jnp.sum(e_ctx, axis=-1, keepdims=True)
                o = (_dotf(e_loc.astype(BF16), vw) + _dotf(e_ctx.astype(BF16), vx)) / den
                acc = o if acc is None else jnp.where(msk, o, acc)
            yc_scr[p, pl.ds(hf * nq, nq), :] = acc
        return carry

    for p in range(n_pairs):
        pair_body(p, 0)
    y_c = jnp.concatenate([yc_scr[p] for p in range(n_pairs)], axis=1) * _silu(proj(_OFF_CG, w))

    o_ref[0] = _merge_tail(x, u, gate, (y_a, y_b, y_c), w2_ref, b2_ref, wbr_ref, wout_ref,
                           og_ref[...], ob_ref[...], alpha)


def _main_call(alpha, x, yb, kv, kx, vx, mod, p, tabs):
    bsz, seq, d = x.shape
    w = W_BRANCH
    tl = MAIN_ROWS * GRID_W
    half = tl // 2
    nt = seq // tl
    n_half = seq // half
    lc = kx.shape[1]

    consts = [mod, p["w2"], p["b2"], p["sgu_g"], p["sgu_b"], p["ws"], p["bs_tab"], p["wbr"], p["wout"],
              p["ln_g"], p["ln_b"]]

    def window_specs(j):
        return [pl.BlockSpec((1, 1, half, w), lambda b, i: (j, b, jnp.maximum(2 * i - 1, 0), 0)),
                pl.BlockSpec((1, 1, tl, w), lambda b, i: (j, b, i, 0)),
                pl.BlockSpec((1, 1, half, w), lambda b, i: (j, b, jnp.minimum(2 * i + 2, n_half - 1), 0))]

    ctx_spec = pl.BlockSpec((1, lc, w), lambda b, i: (b, 0, 0))
    tab_spec = pl.BlockSpec((tl, LANES), lambda b, i: (i, 0))
    in_specs = ([pl.BlockSpec((1, tl, d), lambda b, i: (b, i, 0)),
                 pl.BlockSpec((1, tl, w), lambda b, i: (b, i, 0))]
                + window_specs(0) + window_specs(1) + [ctx_spec, ctx_spec]
                + [_const_spec(a.shape) for a in consts]
                + [tab_spec, tab_spec, tab_spec, _const_spec(tabs["bias"].shape),
                   _const_spec(tabs["row_mask"].shape)])
    n_pairs = w // LANES
    scratch = [pltpu.VMEM((n_pairs, 2 * tl, LANES), BF16), pltpu.VMEM((n_pairs, 2 * tl, LANES), BF16),
               pltpu.VMEM((n_pairs, tl, LANES), BF16), pltpu.VMEM((n_pairs, tl, LANES), F32)]
    return pl.pallas_call(
        functools.partial(_main_kernel, alpha),
        grid=(bsz, nt),
        in_specs=in_specs,
        out_specs=pl.BlockSpec((1, tl, d), lambda b, i: (b, i, 0)),
        out_shape=jax.ShapeDtypeStruct((bsz, seq, d), F32),
        scratch_shapes=scratch,
        compiler_params=pltpu.CompilerParams(dimension_semantics=("arbitrary", "arbitrary"),
                                             vmem_limit_bytes=VMEM_LIMIT),
        name="latent_main",
    )(x, yb, kv, kv, kv, kv, kv, kv, kx, vx, *consts, tabs["cos"], tabs["sin_lo"], tabs["sin_hi"], tabs["bias"],
      tabs["row_mask"])


def _block_diag_groups(wg):
    per = GATE_GROUP // B_BW
    g = B_BLOCKS // per
    wg = wg.reshape(2, g, per, B_BW, B_BW)
    eye = jnp.eye(per, dtype=wg.dtype)
    return jnp.einsum("dghij,hk->dghikj", wg, eye).reshape(2, g, GATE_GROUP, GATE_GROUP)


def _rope_tables(seq):
    pos = jnp.arange(seq)
    rows, cols = (pos // GRID_W).astype(F32), (pos % GRID_W).astype(F32)
    quarter = C_HD // 4
    inv_freq = ROPE_BASE ** (-jnp.arange(quarter, dtype=F32) / quarter)
    lane = np.arange(LANES) % C_HD
    use_row = lane < C_HD // 2
    first = (lane % (C_HD // 2)) < quarter
    inv_lane = jnp.concatenate([inv_freq] * (LANES // quarter))
    ang = jnp.where(use_row[None, :], rows[:, None], cols[:, None]) * inv_lane[None, :]
    sin = jnp.sin(ang)
    return {"cos": jnp.cos(ang),
            "sin_lo": jnp.where(first[None, :], -sin, 0.0),
            "sin_hi": jnp.where(first[None, :], 0.0, sin)}


def _bias_table(rpb_l):
    qc = np.arange(GRID_W)
    c0 = np.clip(qc - WIN_C // 2, 0, GRID_W - WIN_C)
    kc = np.arange(GRID_W)
    allowed = (kc[None, :] >= c0[:, None]) & (kc[None, :] < c0[:, None] + WIN_C)
    cidx = np.clip(kc[None, :] - qc[:, None] + (WIN_C - 1), 0, 2 * WIN_C - 2)
    t = None
    for cc in range(2 * WIN_C - 1):
        sel = jnp.asarray((cidx == cc) & allowed, F32)
        term = rpb_l[:, :, cc][:, :, None, None] * sel[None, None]
        t = term if t is None else t + term
    t = jnp.where(allowed[None, None], t, NEG_BIG)
    return jnp.concatenate([t[:, :-1], t[:, 1:]], axis=-1)


def _row_mask_table():
    lane_first = np.arange(LANES) < GRID_W
    tab = np.zeros((4, GRID_W, LANES), np.float32)
    for a in range(2):
        for b in range(2):
            ok = np.where(lane_first, bool(a), bool(b))
            tab[2 * a + b] = np.where(ok, 0.0, NEG_BIG)[None, :]
    return jnp.asarray(tab)


def _layer_params(l, w_in, b_in, sgu_ln_g, sgu_ln_b, w_s, b_s, conv_w, conv_b, lru_wa, lru_ba, lru_wx, lru_bx,
                  lru_lam, w_br, w_out, ln_g, ln_b):
    w = W_BRANCH
    wi, bi = w_in[l], b_in[l]
    cols1 = lambda a: jnp.concatenate([a[..., 3 * w:4 * w], a[..., 6 * w:8 * w]], axis=-1)
    cols2 = lambda a: jnp.concatenate([a[..., 0:3 * w], a[..., 4 * w:6 * w], a[..., 8 * w:]], axis=-1)
    gw = jnp.concatenate([_block_diag_groups(lru_wa[l]), _block_diag_groups(lru_wx[l])], axis=-1)
    return {
        "w1": cols1(wi).astype(BF16), "b1": cols1(bi)[None, :],
        "w2": cols2(wi).astype(BF16), "b2": cols2(bi)[None, :],
        "conv_w": conv_w[l], "conv_b": conv_b[l][None, :],
        "gw": gw.astype(BF16), "gba": lru_ba[l][:, None, :], "gbx": lru_bx[l][:, None, :],
        "lam": lru_lam[l][:, None, :],
        "sgu_g": sgu_ln_g[l][None, :], "sgu_b": sgu_ln_b[l][None, :],
        "ws": w_s[l].astype(BF16), "bs_tab": jnp.repeat(b_s[l].T, LANES, axis=1),
        "wbr": w_br[l].astype(BF16), "wout": w_out[l].astype(BF16),
        "ln_g": ln_g[l][None, :], "ln_b": ln_b[l][None, :],
    }


def kernel(x, c, ctx, c_ctx, w_ada, b_ada, w_in, b_in, sgu_ln_g, sgu_ln_b, w_s, b_s, conv_w, conv_b, lru_wa,
           lru_ba, lru_wx, lru_bx, lru_lam, rpb, w_br, w_out, ln_g, ln_b):
    bsz, seq, d = x.shape
    depth = w_ada.shape[0]
    assert d % LANES == 0 and seq % (MAIN_ROWS * GRID_W) == 0 and seq // GRID_W >= 2 * MAIN_ROWS
    assert seq % SCAN_TILE == 0 and ctx.shape[1] % CHUNK == 0 and bsz <= 8
    alpha = (2 * depth) ** 0.25

    pad = jnp.zeros((16 - bsz - 1, d), F32)
    mods = _ada_call(jnp.concatenate([c, c_ctx[None, :], pad], axis=0), w_ada, b_ada)
    tabs = _rope_tables(seq)

    xc = ctx
    for l in range(depth):
        with_ctx = l < depth - 1
        p = _layer_params(l, w_in, b_in, sgu_ln_g, sgu_ln_b, w_s, b_s, conv_w, conv_b, lru_wa, lru_ba, lru_wx,
                          lru_bx, lru_lam, w_br, w_out, ln_g, ln_b)
        ctx_out = _ctx_call(with_ctx, alpha, xc, mods[l], p)
        kx, vx, hfin = ctx_out[0], ctx_out[1], ctx_out[2]
        h0 = hfin.transpose(1, 0, 2)
        yb, kv = _scan_calls(x, mods[l], h0, p, tabs)
        layer_tabs = dict(tabs, bias=_bias_table(rpb[l]) * LOG2E, row_mask=_row_mask_table())
        x = _main_call(alpha, x, yb, kv, kx, vx, mods[l], p, layer_tabs)
        if with_ctx:
            xc = ctx_out[3]
    return x
```
